```python
import jax, jax.numpy as jnp
from jax import lax
import numpy as np


D_MODEL = 1024
BATCH = 8
SEQ = 8192
DEPTH = 2

N_MIXERS = 2
N_ATTN_LAYERS = (DEPTH + 1) // 2
N_CONV_LAYERS = DEPTH // 2
N_HEADS = 8
QK_NOPE_DIM = 128
QK_ROPE_DIM = 64
QK_HEAD_DIM = QK_NOPE_DIM + QK_ROPE_DIM
V_HEAD_DIM = 128
Q_LORA_RANK = 768
KV_LORA_RANK = 256
ROPE_THETA = 10000.0
Q_BLOCK = 128
CONV_WIDTH = 31
CONV_DIM = D_MODEL
N_EXPERTS = 16
N_GROUPS = 4
EXPERTS_PER_GROUP = N_EXPERTS // N_GROUPS
TOP_K = 2
EXPERT_FF = 512
NORM_EPS = 1e-6

kernel_name = 'hybrid_mla_conformer_grouped_moe'


def rms_norm(x, g):
    xf = x.astype(jnp.float32)
    y = xf * lax.rsqrt(jnp.mean(xf * xf, axis=-1, keepdims=True) + NORM_EPS)
    return y.astype(x.dtype) * g


def layer_norm(x, g, b):
    xf = x.astype(jnp.float32)
    mu = jnp.mean(xf, axis=-1, keepdims=True)
    var = jnp.mean(jnp.square(xf - mu), axis=-1, keepdims=True)
    y = (xf - mu) * lax.rsqrt(var + NORM_EPS)
    return y.astype(x.dtype) * g + b


def rope(x, positions):
    half = QK_ROPE_DIM // 2
    inv_freq = 1.0 / (ROPE_THETA ** (jnp.arange(half, dtype=jnp.float32) * (2.0 / QK_ROPE_DIM)))
    ang = positions.astype(jnp.float32)[..., None] * inv_freq
    cos = jnp.cos(ang)[:, :, None, :]
    sin = jnp.sin(ang)[:, :, None, :]
    x1 = x[..., :half].astype(jnp.float32)
    x2 = x[..., half:].astype(jnp.float32)
    out = jnp.concatenate([x1 * cos - x2 * sin, x2 * cos + x1 * sin], axis=-1)
    return out.astype(x.dtype)


def causal_block_attention(q, k, v):
    b, s, h, dqk = q.shape
    nb = s // Q_BLOCK
    scale = QK_HEAD_DIM ** -0.5
    q_blocks = q.reshape(b, nb, Q_BLOCK, h, dqk).transpose(1, 0, 2, 3, 4)
    k_pos = jnp.arange(s)

    def one_block(args):
        qb, blk = args
        q_pos = blk * Q_BLOCK + jnp.arange(Q_BLOCK)
        scores = jnp.einsum('bqhd,bkhd->bhqk', qb, k, preferred_element_type=jnp.float32) * scale
        mask = k_pos[None, :] <= q_pos[:, None]
        scores = jnp.where(mask[None, None], scores, -jnp.inf)
        p = jax.nn.softmax(scores, axis=-1)
        return jnp.einsum('bhqk,bkhd->bqhd', p.astype(v.dtype), v)

    out = lax.map(one_block, (q_blocks, jnp.arange(nb)))
    return out.transpose(1, 0, 2, 3, 4).reshape(b, s, h, v.shape[-1])


def mla_mixer(h, positions, w_dqkv, q_norm_g, w_uq, kv_norm_g, w_ukv, qk_q_g, qk_k_g, w_o):
    b, s, _ = h.shape
    lat = h @ w_dqkv
    c_q = rms_norm(lat[..., :Q_LORA_RANK], q_norm_g)
    c_kv = rms_norm(lat[..., Q_LORA_RANK:Q_LORA_RANK + KV_LORA_RANK], kv_norm_g)
    k_pe = lat[..., Q_LORA_RANK + KV_LORA_RANK:]
    q = (c_q @ w_uq).reshape(b, s, N_HEADS, QK_HEAD_DIM)
    kv = (c_kv @ w_ukv).reshape(b, s, N_HEADS, QK_NOPE_DIM + V_HEAD_DIM)
    k_nope, v = kv[..., :QK_NOPE_DIM], kv[..., QK_NOPE_DIM:]
    k = jnp.concatenate([k_nope, jnp.broadcast_to(k_pe[:, :, None, :], (b, s, N_HEADS, QK_ROPE_DIM))], axis=-1)
    q = rms_norm(q, qk_q_g)
    k = rms_norm(k, qk_k_g)
    q = jnp.concatenate([q[..., :QK_NOPE_DIM], rope(q[..., QK_NOPE_DIM:], positions)], axis=-1)
    k = jnp.concatenate([k[..., :QK_NOPE_DIM], rope(k[..., QK_NOPE_DIM:], positions)], axis=-1)
    o = causal_block_attention(q, k, v)
    return o.reshape(b, s, N_HEADS * V_HEAD_DIM) @ w_o


def conformer_conv_mixer(h, pw1_w, pw1_b, dw_w, dw_b, ln_g, ln_b, pw2_w, pw2_b):
    u = h @ pw1_w + pw1_b
    u = u[..., :CONV_DIM] * jax.nn.sigmoid(u[..., CONV_DIM:])
    u = lax.conv_general_dilated(
        u, dw_w[:, None, :], window_strides=(1,), padding=[(CONV_WIDTH - 1, 0)],
        dimension_numbers=('NWC', 'WIO', 'NWC'), feature_group_count=CONV_DIM) + dw_b
    u = jax.nn.silu(layer_norm(u, ln_g, ln_b))
    return u @ pw2_w + pw2_b


def route(h, router_w, router_bias):
    b, s, _ = h.shape
    logits = jnp.einsum('bsd,de->bse', h.astype(jnp.float32), router_w.astype(jnp.float32))
    scores = jax.nn.sigmoid(logits)
    biased = scores + router_bias.astype(jnp.float32)
    grouped = biased.reshape(b, s, N_GROUPS, EXPERTS_PER_GROUP)
    group_score = jnp.sum(lax.top_k(grouped, TOP_K)[0], axis=-1)
    g_sel = jnp.argmax(group_score, axis=-1)
    g_idx = jnp.broadcast_to(g_sel[..., None, None], (b, s, 1, EXPERTS_PER_GROUP))
    in_group = jnp.take_along_axis(grouped, g_idx, axis=2)[:, :, 0, :]
    _, local_idx = lax.top_k(in_group, TOP_K)
    expert_idx = g_sel[..., None] * EXPERTS_PER_GROUP + local_idx
    w = jnp.take_along_axis(scores, expert_idx, axis=-1)
    w = w / jnp.sum(w, axis=-1, keepdims=True)
    gates = jnp.sum(jax.nn.one_hot(expert_idx, N_EXPERTS, dtype=jnp.float32) * w[..., None], axis=-2)
    return gates


def moe_ffn(h, gates, w_gate, w_up, w_down):
    out = jnp.zeros_like(h)
    for e in range(N_EXPERTS):
        a = jax.nn.silu(h @ w_gate[e]) * (h @ w_up[e])
        out = out + gates[..., e:e + 1].astype(h.dtype) * (a @ w_down[e])
    return out


def setup_inputs(seed: int = 0) -> dict:
    key = jax.random.key(seed)
    ks = jax.random.split(key, 32)
    f32 = jnp.float32
    D = D_MODEL

    def nrm(k, shape, fan_in, mult=1.0):
        return jax.random.normal(k, shape, f32) * (mult * fan_in ** -0.5)

    def gain(k, shape):
        return 1.0 + 0.05 * jax.random.normal(k, shape, f32)

    def bias(k, shape):
        return 0.01 * jax.random.normal(k, shape, f32)

    x = jax.random.normal(ks[0], (BATCH, SEQ, D), f32)
    c = jax.random.normal(ks[1], (BATCH, D), f32)
    offsets = jax.random.randint(ks[2], (BATCH, 1), 0, 1024, dtype=jnp.int32)
    positions = (offsets + jnp.arange(SEQ, dtype=jnp.int32)[None, :]).astype(jnp.int32)
    NA, NC = N_ATTN_LAYERS, N_CONV_LAYERS
    return {
        'x': x,
        'c': c,
        'positions': positions,
        'ada_w': nrm(ks[3], (DEPTH, D, 6 * D), D, 0.5),
        'ada_b': bias(ks[4], (DEPTH, 6 * D)),
        'norm1_g': gain(ks[5], (DEPTH, D)),
        'norm2_g': gain(ks[6], (DEPTH, D)),
        'mla_w_dqkv': nrm(ks[7], (NA, D, Q_LORA_RANK + KV_LORA_RANK + QK_ROPE_DIM), D),
        'mla_q_norm_g': gain(ks[8], (NA, Q_LORA_RANK)),
        'mla_w_uq': nrm(ks[9], (NA, Q_LORA_RANK, N_HEADS * QK_HEAD_DIM), Q_LORA_RANK),
        'mla_kv_norm_g': gain(ks[10], (NA, KV_LORA_RANK)),
        'mla_w_ukv': nrm(ks[11], (NA, KV_LORA_RANK, N_HEADS * (QK_NOPE_DIM + V_HEAD_DIM)), KV_LORA_RANK),
        'mla_qk_q_g': gain(ks[12], (NA, QK_HEAD_DIM)),
        'mla_qk_k_g': gain(ks[13], (NA, QK_HEAD_DIM)),
        'mla_w_o': nrm(ks[14], (NA, N_HEADS * V_HEAD_DIM, D), N_HEADS * V_HEAD_DIM),
        'conv_pw1_w': nrm(ks[15], (NC, D, 2 * CONV_DIM), D),
        'conv_pw1_b': bias(ks[16], (NC, 2 * CONV_DIM)),
        'conv_dw_w': nrm(ks[17], (NC, CONV_WIDTH, CONV_DIM), CONV_WIDTH),
        'conv_dw_b': bias(ks[18], (NC, CONV_DIM)),
        'conv_ln_g': gain(ks[19], (NC, CONV_DIM)),
        'conv_ln_b': bias(ks[20], (NC, CONV_DIM)),
        'conv_pw2_w': nrm(ks[21], (NC, CONV_DIM, D), CONV_DIM),
        'conv_pw2_b': bias(ks[22], (NC, D)),
        'router_w': nrm(ks[23], (D, N_EXPERTS), D),
        'router_bias': bias(ks[24], (N_EXPERTS,)),
        'moe_w_gate': nrm(ks[25], (DEPTH, N_EXPERTS, D, EXPERT_FF), D),
        'moe_w_up': nrm(ks[26], (DEPTH, N_EXPERTS, D, EXPERT_FF), D),
        'moe_w_down': nrm(ks[27], (DEPTH, N_EXPERTS, EXPERT_FF, D), EXPERT_FF),
    }


def reference(x, c, positions, ada_w, ada_b, norm1_g, norm2_g,
              mla_w_dqkv, mla_q_norm_g, mla_w_uq, mla_kv_norm_g, mla_w_ukv,
              mla_qk_q_g, mla_qk_k_g, mla_w_o,
              conv_pw1_w, conv_pw1_b, conv_dw_w, conv_dw_b, conv_ln_g, conv_ln_b,
              conv_pw2_w, conv_pw2_b,
              router_w, router_bias, moe_w_gate, moe_w_up, moe_w_down):
    mod_all = jnp.einsum('bd,lde->lbe', jax.nn.silu(c), ada_w) + ada_b[:, None, :]
    for i in range(DEPTH):
        shift1, scale1, gate1, shift2, scale2, gate2 = jnp.split(mod_all[i][:, None, :], 6, axis=-1)
        h = rms_norm(x, norm1_g[i]) * (1 + scale1) + shift1
        j = i // N_MIXERS
        if i % N_MIXERS == 0:
            y = mla_mixer(h, positions, mla_w_dqkv[j], mla_q_norm_g[j], mla_w_uq[j],
                          mla_kv_norm_g[j], mla_w_ukv[j], mla_qk_q_g[j], mla_qk_k_g[j], mla_w_o[j])
        else:
            y = conformer_conv_mixer(h, conv_pw1_w[j], conv_pw1_b[j], conv_dw_w[j], conv_dw_b[j],
                                     conv_ln_g[j], conv_ln_b[j], conv_pw2_w[j], conv_pw2_b[j])
        x = x + gate1 * y
        h = rms_norm(x, norm2_g[i]) * (1 + scale2) + shift2
        gates = route(h, router_w, router_bias)
        x = x + gate2 * moe_ffn(h, gates, moe_w_gate[i], moe_w_up[i], moe_w_down[i])
    return x
```

```python
import functools

import jax
import jax.numpy as jnp
from jax import lax
from jax.experimental import pallas as pl
from jax.experimental.pallas import tpu as pltpu

F32 = jnp.float32
BF16 = jnp.bfloat16

N_HEADS = 8
QK_NOPE = 128
QK_ROPE = 64
QK_HEAD = QK_NOPE + QK_ROPE
V_HEAD = 128
Q_LORA = 768
KV_LORA = 256
ROPE_THETA = 10000.0
CONV_WIDTH = 31
CONV_HALO = 32
N_EXPERTS = 16
N_GROUPS = 4
EXPERTS_PER_GROUP = N_EXPERTS // N_GROUPS
TOP_K = 2
NORM_EPS = 1e-6
LANES = 128
VMEM_LIMIT_BYTES = 52 * 1024 * 1024


def _params(*semantics):
    return pltpu.CompilerParams(dimension_semantics=semantics, vmem_limit_bytes=VMEM_LIMIT_BYTES)


def _rms(x):
    return x * lax.rsqrt(jnp.mean(x * x, axis=-1, keepdims=True) + NORM_EPS)


def _silu(x):
    return x * jax.nn.sigmoid(x)


def _modulated_norm(x, g, scale, shift):
    return (_rms(x) * g) * (1.0 + scale) + shift


def _adaln_kernel(c_ref, w_ref, b_ref, o_ref):
    a = _silu(c_ref[...]).astype(BF16)
    o_ref[0] = jnp.dot(a, w_ref[0].astype(BF16), preferred_element_type=F32) + b_ref[0]


def _adaln(c, ada_w, ada_b, tn=1536):
    depth, d, n = ada_w.shape
    b = c.shape[0]
    return pl.pallas_call(
        _adaln_kernel,
        grid=(depth, n // tn),
        in_specs=[
            pl.BlockSpec((b, d), lambda l, j: (0, 0)),
            pl.BlockSpec((1, d, tn), lambda l, j: (l, 0, j)),
            pl.BlockSpec((1, 1, tn), lambda l, j: (l, 0, j)),
        ],
        out_specs=pl.BlockSpec((1, b, tn), lambda l, j: (l, 0, j)),
        out_shape=jax.ShapeDtypeStruct((depth, b, n), F32),
        compiler_params=_params("arbitrary", "arbitrary"),
        name="adaln",
    )(c, ada_w, ada_b.reshape(depth, 1, n))


def _rope128(x, cos, sin_signed, lo_mask):
    swapped = jnp.where(lo_mask, pltpu.roll(x, LANES - QK_ROPE // 2, 1), pltpu.roll(x, QK_ROPE // 2, 1))
    return x * cos + swapped * sin_signed


def _mla_proj_kernel(x_ref, mod_ref, n1g_ref, wd_ref, qg_ref, wuq_ref, kvg_ref, wukv_ref,
                     gq_ref, gk_ref, pos_ref, invf_ref, q_out, k_out, v_out):
    x = x_ref[0]
    h = _modulated_norm(x, n1g_ref[...], mod_ref[0, 1:2, :], mod_ref[0, 0:1, :]).astype(BF16)
    lat = jnp.dot(h, wd_ref[...], preferred_element_type=F32)
    c_q = (_rms(lat[:, :Q_LORA]) * qg_ref[...]).astype(BF16)
    c_kv = (_rms(lat[:, Q_LORA:Q_LORA + KV_LORA]) * kvg_ref[...]).astype(BF16)
    k_pe = lat[:, Q_LORA + KV_LORA:]
    q = jnp.dot(c_q, wuq_ref[...], preferred_element_type=F32)
    kv = jnp.dot(c_kv, wukv_ref[...], preferred_element_type=F32)

    ang = pos_ref[0].astype(F32) * invf_ref[...]
    cos = jnp.cos(ang)
    lane = lax.broadcasted_iota(jnp.int32, (1, LANES), 1)
    lo_mask = (lane % QK_ROPE) < (QK_ROPE // 2)
    sin_signed = jnp.where(lo_mask, -1.0, 1.0) * jnp.sin(ang)

    gq = gq_ref[...]
    gk = gk_ref[...]
    gq_rope2 = jnp.concatenate([gq[:, QK_NOPE:], gq[:, QK_NOPE:]], axis=-1)
    gk_rope2 = jnp.concatenate([gk[:, QK_NOPE:], gk[:, QK_NOPE:]], axis=-1)
    sm_scale = QK_HEAD ** -0.5
    nope_all = N_HEADS * QK_NOPE

    k_pe2 = jnp.concatenate([k_pe, k_pe], axis=-1)
    k_rope = _rope128(k_pe2 * gk_rope2, cos, sin_signed, lo_mask)[:, :QK_ROPE]
    ss_pe = jnp.sum(k_pe * k_pe, axis=-1, keepdims=True)

    for pair in range(N_HEADS // 2):
        q_rope_pair = q[:, nope_all + pair * LANES: nope_all + (pair + 1) * LANES]
        q_roped_pair = _rope128(q_rope_pair * gq_rope2, cos, sin_signed, lo_mask)
        for sub in range(2):
            hd = 2 * pair + sub
            qn = q[:, hd * QK_NOPE:(hd + 1) * QK_NOPE]
            qr = q_rope_pair[:, sub * QK_ROPE:(sub + 1) * QK_ROPE]
            ss = jnp.sum(qn * qn, axis=-1, keepdims=True) + jnp.sum(qr * qr, axis=-1, keepdims=True)
            r = lax.rsqrt(ss * (1.0 / QK_HEAD) + NORM_EPS) * sm_scale
            q_out[0, hd, :, 0:QK_NOPE] = (qn * r * gq[:, :QK_NOPE]).astype(BF16)
            q_out[0, hd, :, QK_NOPE:QK_HEAD] = (
                q_roped_pair[:, sub * QK_ROPE:(sub + 1) * QK_ROPE] * r).astype(BF16)

            kn = kv[:, hd * QK_NOPE:(hd + 1) * QK_NOPE]
            rk = lax.rsqrt((jnp.sum(kn * kn, axis=-1, keepdims=True) + ss_pe) * (1.0 / QK_HEAD) + NORM_EPS)
            k_out[0, hd, :, 0:QK_NOPE] = (kn * rk * gk[:, :QK_NOPE]).astype(BF16)
            k_out[0, hd, :, QK_NOPE:QK_HEAD] = (k_rope * rk).astype(BF16)
            v_out[0, hd] = kv[:, nope_all + hd * V_HEAD: nope_all + (hd + 1) * V_HEAD].astype(BF16)


def _mla_proj(x, mod, n1g, wd, qg, wuq, kvg, wukv, gq, gk, positions, ts=512):
    b, s, d = x.shape
    half = QK_ROPE // 2
    inv_freq = 1.0 / (ROPE_THETA ** (jnp.arange(half, dtype=F32) * (2.0 / QK_ROPE)))
    invf = jnp.tile(inv_freq, LANES // half).reshape(1, LANES)
    const = lambda i, j: (0, 0)
    return pl.pallas_call(
        _mla_proj_kernel,
        grid=(b, s // ts),
        in_specs=[
            pl.BlockSpec((1, ts, d), lambda i, j: (i, j, 0)),
            pl.BlockSpec((1, 6, d), lambda i, j: (i, 0, 0)),
            pl.BlockSpec((1, d), const),
            pl.BlockSpec(wd.shape, const),
            pl.BlockSpec((1, Q_LORA), const),
            pl.BlockSpec(wuq.shape, const),
            pl.BlockSpec((1, KV_LORA), const),
            pl.BlockSpec(wukv.shape, const),
            pl.BlockSpec((1, QK_HEAD), const),
            pl.BlockSpec((1, QK_HEAD), const),
            pl.BlockSpec((1, ts, 1), lambda i, j: (i, j, 0)),
            pl.BlockSpec((1, LANES), const),
        ],
        out_specs=[
            pl.BlockSpec((1, N_HEADS, ts, QK_HEAD), lambda i, j: (i, 0, j, 0)),
            pl.BlockSpec((1, N_HEADS, ts, QK_HEAD), lambda i, j: (i, 0, j, 0)),
            pl.BlockSpec((1, N_HEADS, ts, V_HEAD), lambda i, j: (i, 0, j, 0)),
        ],
        out_shape=[
            jax.ShapeDtypeStruct((b, N_HEADS, s, QK_HEAD), BF16),
            jax.ShapeDtypeStruct((b, N_HEADS, s, QK_HEAD), BF16),
            jax.ShapeDtypeStruct((b, N_HEADS, s, V_HEAD), BF16),
        ],
        compiler_params=_params("arbitrary", "arbitrary"),
        name="mla_proj",
    )(x, mod, n1g.reshape(1, d), wd, qg.reshape(1, -1), wuq, kvg.reshape(1, -1), wukv,
      gq.reshape(1, -1), gk.reshape(1, -1), positions.reshape(b, s, 1), invf)


def _flash_kernel(q_ref, k_ref, v_ref, o_ref, *, tq, tk):
    qi = pl.program_id(2)
    q = q_ref[0, 0]
    diag_blocks = tq // tk

    def step(kb, carry, masked):
        m, l, acc = carry
        start = pl.multiple_of(kb * tk, tk)
        k = k_ref[0, 0, pl.ds(start, tk), :]
        v = v_ref[0, 0, pl.ds(start, tk), :]
        s = lax.dot_general(q, k, (((1,), (1,)), ((), ())), preferred_element_type=F32)
        if masked:
            row = qi * tq + lax.broadcasted_iota(jnp.int32, (tq, tk), 0)
            col = kb * tk + lax.broadcasted_iota(jnp.int32, (tq, tk), 1)
            s = jnp.where(col <= row, s, -jnp.inf)
        m_new = jnp.maximum(m, jnp.max(s, axis=-1, keepdims=True))
        p = jnp.exp(s - m_new)
        alpha = jnp.exp(m - m_new)
        l_new = alpha * l + jnp.sum(p, axis=-1, keepdims=True)
        acc_new = alpha * acc + jnp.dot(p.astype(BF16), v, preferred_element_type=F32)
        return m_new, l_new, acc_new

    init = (jnp.full((tq, 1), -jnp.inf, F32), jnp.zeros((tq, 1), F32), jnp.zeros((tq, V_HEAD), F32))
    carry = lax.fori_loop(0, qi * diag_blocks, lambda kb, c: step(kb, c, False), init)
    for d in range(diag_blocks):
        carry = step(qi * diag_blocks + d, carry, True)
    _, l, acc = carry
    o_ref[0] = (acc * (1.0 / l)).astype(BF16)


def _flash_attention(q, k, v, tq=512, tk=512):
    b, h, s, dqk = q.shape
    dv = v.shape[-1]
    return pl.pallas_call(
        functools.partial(_flash_kernel, tq=tq, tk=tk),
        grid=(b, h, s // tq),
        in_specs=[
            pl.BlockSpec((1, 1, tq, dqk), lambda i, j, t: (i, j, t, 0)),
            pl.BlockSpec((1, 1, s, dqk), lambda i, j, t: (i, j, 0, 0)),
            pl.BlockSpec((1, 1, s, dv), lambda i, j, t: (i, j, 0, 0)),
        ],
        out_specs=pl.BlockSpec((1, tq, dv), lambda i, j, t: (i, t, j)),
        out_shape=jax.ShapeDtypeStruct((b, s, h * dv), BF16),
        compiler_params=_params("arbitrary", "arbitrary", "arbitrary"),
        name="flash_attention",
    )(q, k, v)


def _route_rows(logits, bias_col):
    scores = jax.nn.sigmoid(logits)
    biased = scores + bias_col
    b_rows = [biased[i:i + 1, :] for i in range(N_EXPERTS)]
    s_rows = [scores[i:i + 1, :] for i in range(N_EXPERTS)]
    in_top = []
    group_score = []
    for g in range(N_GROUPS):
        grp = b_rows[g * EXPERTS_PER_GROUP:(g + 1) * EXPERTS_PER_GROUP]
        gs = None
        for i in range(EXPERTS_PER_GROUP):
            rank = None
            for j in range(EXPERTS_PER_GROUP):
                if j == i:
                    continue
                beats = (grp[j] >= grp[i]) if j < i else (grp[j] > grp[i])
                cnt = jnp.where(beats, 1.0, 0.0)
                rank = cnt if rank is None else rank + cnt
            sel = rank < float(TOP_K)
            in_top.append(sel)
            term = jnp.where(sel, grp[i], 0.0)
            gs = term if gs is None else gs + term
        group_score.append(gs)
    picked = []
    for g in range(N_GROUPS):
        lost = None
        for g2 in range(N_GROUPS):
            if g2 == g:
                continue
            beats = (group_score[g2] >= group_score[g]) if g2 < g else (group_score[g2] > group_score[g])
            cnt = jnp.where(beats, 1.0, 0.0)
            lost = cnt if lost is None else lost + cnt
        picked.append(lost < 1.0)
    w_rows = []
    denom = None
    for i in range(N_EXPERTS):
        sel = jnp.where(picked[i // EXPERTS_PER_GROUP], jnp.where(in_top[i], 1.0, 0.0), 0.0)
        w = sel * s_rows[i]
        w_rows.append(w)
        denom = w if denom is None else denom + w
    inv = 1.0 / denom
    return jnp.concatenate([w * inv for w in w_rows], axis=0)


def _split_bf16(x):
    hi = x.astype(BF16)
    lo = (x - hi.astype(F32)).astype(BF16)
    return hi, lo


def _mixer_tail(x, y, mod_ref, n2g, rw_ref, rb_ref, x_out, h_out, g_out):
    x1 = x + mod_ref[0, 2:3, :] * y
    x_out[0] = x1
    h2 = _modulated_norm(x1, n2g, mod_ref[0, 4:5, :], mod_ref[0, 3:4, :])
    h_hi, h_lo = _split_bf16(h2)
    h_out[0] = h_hi
    nt = (((1,), (1,)), ((), ()))
    a = lax.dot_general(rw_ref[...], h_hi, nt, preferred_element_type=F32)
    c = lax.dot_general(rw_ref[0:N_EXPERTS, :], h_lo, nt, preferred_element_type=F32)
    logits = a[0:N_EXPERTS, :] + a[N_EXPERTS:, :] + c
    g_out[...] = _route_rows(logits, rb_ref[...])


def _attn_out_kernel(o_ref, x_ref, mod_ref, wo_ref, n2g_ref, rw_ref, rb_ref, x_out, h_out, g_out):
    y = jnp.dot(o_ref[0], wo_ref[...], preferred_element_type=F32)
    _mixer_tail(x_ref[0], y, mod_ref, n2g_ref[...], rw_ref, rb_ref, x_out, h_out, g_out)


def _tail_out_specs(b, s, d, tm):
    nt = s // tm
    specs = [
        pl.BlockSpec((1, tm, d), lambda i, j: (i, j, 0)),
        pl.BlockSpec((1, tm, d), lambda i, j: (i, j, 0)),
        pl.BlockSpec((N_EXPERTS, tm), lambda i, j: (0, i * nt + j)),
    ]
    shapes = [
        jax.ShapeDtypeStruct((b, s, d), F32),
        jax.ShapeDtypeStruct((b, s, d), BF16),
        jax.ShapeDtypeStruct((N_EXPERTS, b * s), F32),
    ]
    return specs, shapes


def _attn_out(o, x, mod, wo, n2g, rw, rb, tm=512):
    b, s, d = x.shape
    const = lambda i, j: (0, 0)
    out_specs, out_shape = _tail_out_specs(b, s, d, tm)
    return pl.pallas_call(
        _attn_out_kernel,
        grid=(b, s // tm),
        in_specs=[
            pl.BlockSpec((1, tm, o.shape[-1]), lambda i, j: (i, j, 0)),
            pl.BlockSpec((1, tm, d), lambda i, j: (i, j, 0)),
            pl.BlockSpec((1, 6, d), lambda i, j: (i, 0, 0)),
            pl.BlockSpec(wo.shape, const),
            pl.BlockSpec((1, d), const),
            pl.BlockSpec(rw.shape, const),
            pl.BlockSpec((N_EXPERTS, 1), const),
        ],
        out_specs=out_specs,
        out_shape=out_shape,
        compiler_params=_params("arbitrary", "arbitrary"),
        name="attn_out_router",
    )(o, x, mod, wo, n2g.reshape(1, d), rw, rb)


def _conv_mixer_kernel(x_ref, mod_ref, n1g_ref, w1_ref, b1_ref, dww_ref, dwb_ref, lng_ref, lnb_ref,
                       w2_ref, b2_ref, n2g_ref, rw_ref, rb_ref, x_out, h_out, g_out, ubuf, vbuf, cbuf,
                       *, ts, rows):
    j = pl.program_id(1)
    x = x_ref[0]
    d = x.shape[-1]
    h = _modulated_norm(x, n1g_ref[...], mod_ref[0, 1:2, :], mod_ref[0, 0:1, :]).astype(BF16)
    u = jnp.dot(h, w1_ref[...], preferred_element_type=F32) + b1_ref[...]
    u = u[:, :d] * jax.nn.sigmoid(u[:, d:])

    nc = d // LANES

    @pl.when(j == 0)
    def _():
        ubuf[:, 0:CONV_HALO, :] = jnp.zeros((nc, CONV_HALO, LANES), F32)

    @pl.when(j > 0)
    def _():
        ubuf[:, 0:CONV_HALO, :] = ubuf[:, ts:ts + CONV_HALO, :]

    for cc in range(nc):
        ubuf[cc, CONV_HALO:CONV_HALO + ts, :] = u[:, cc * LANES:(cc + 1) * LANES]

    first = CONV_HALO - (CONV_WIDTH - 1)

    def conv_chunk(cc, carry):
        for r in range(8):
            span = ts + 8 * ((CONV_WIDTH - 1 - r) // 8)
            vbuf[r, 0:span, :] = ubuf[cc, first + r:first + r + span, :]
        for i in range(ts // rows):
            acc = None
            for tap in range(CONV_WIDTH):
                r0 = i * rows + 8 * (tap // 8)
                term = dww_ref[cc, tap:tap + 1, :] * vbuf[tap % 8, r0:r0 + rows, :]
                acc = term if acc is None else acc + term
            cbuf[cc, i * rows:(i + 1) * rows, :] = acc
        return carry

    lax.fori_loop(0, nc, conv_chunk, 0)
    v = jnp.concatenate([cbuf[cc] for cc in range(nc)], axis=-1) + dwb_ref[...]
    mu = jnp.mean(v, axis=-1, keepdims=True)
    vc = v - mu
    var = jnp.mean(vc * vc, axis=-1, keepdims=True)
    v = _silu(vc * lax.rsqrt(var + NORM_EPS) * lng_ref[...] + lnb_ref[...])
    y = jnp.dot(v.astype(BF16), w2_ref[...], preferred_element_type=F32) + b2_ref[...]
    _mixer_tail(x, y, mod_ref, n2g_ref[...], rw_ref, rb_ref, x_out, h_out, g_out)


def _conv_mixer(x, mod, n1g, w1, b1, dww, dwb, lng, lnb, w2, b2, n2g, rw, rb, ts=512, rows=64):
    b, s, d = x.shape
    nc = d // LANES
    const = lambda i, j: (0, 0)
    row = lambda a: a.reshape(1, -1)
    dww = dww.reshape(CONV_WIDTH, nc, LANES).transpose(1, 0, 2)
    out_specs, out_shape = _tail_out_specs(b, s, d, ts)
    return pl.pallas_call(
        functools.partial(_conv_mixer_kernel, ts=ts, rows=rows),
        grid=(b, s // ts),
        in_specs=[
            pl.BlockSpec((1, ts, d), lambda i, j: (i, j, 0)),
            pl.BlockSpec((1, 6, d), lambda i, j: (i, 0, 0)),
            pl.BlockSpec((1, d), const),
            pl.BlockSpec(w1.shape, const),
            pl.BlockSpec((1, 2 * d), const),
            pl.BlockSpec(dww.shape, lambda i, j: (0, 0, 0)),
            pl.BlockSpec((1, d), const),
            pl.BlockSpec((1, d), const),
            pl.BlockSpec((1, d), const),
            pl.BlockSpec(w2.shape, const),
            pl.BlockSpec((1, d), const),
            pl.BlockSpec((1, d), const),
            pl.BlockSpec(rw.shape, const),
            pl.BlockSpec((N_EXPERTS, 1), const),
        ],
        out_specs=out_specs,
        out_shape=out_shape,
        scratch_shapes=[pltpu.VMEM((nc, ts + CONV_HALO, LANES), F32),
                        pltpu.VMEM((8, ts + CONV_HALO, LANES), F32),
                        pltpu.VMEM((nc, ts, LANES), F32)],
        compiler_params=_params("arbitrary", "arbitrary"),
        name="conv_mixer_router",
    )(x, mod, row(n1g), w1, row(b1), dww, row(dwb), row(lng), row(lnb), w2, row(b2), row(n2g), rw, rb)


def _moe_dense_kernel(h_ref, g_ref, x_ref, mod_ref, wgu_ref, wd_ref, o_ref, acc_ref):
    e = pl.program_id(2)
    ff = wd_ref.shape[1]

    @pl.when(e == 0)
    def _():
        acc_ref[...] = jnp.zeros_like(acc_ref)

    gu = jnp.dot(h_ref[0], wgu_ref[0], preferred_element_type=F32)
    a = _silu(gu[:, :ff]) * gu[:, ff:]
    lane = lax.broadcasted_iota(jnp.int32, (1, N_EXPERTS), 1)
    gate = jnp.sum(jnp.where(lane == e, g_ref[0], 0.0), axis=-1, keepdims=True)
    acc_ref[...] += jnp.dot((a * gate).astype(BF16), wd_ref[0], preferred_element_type=F32)

    @pl.when(e == N_EXPERTS - 1)
    def _():
        o_ref[0] = x_ref[0] + mod_ref[0, 5:6, :] * acc_ref[...]


def _moe_dense(h, gates, x, mod, wgu, wd, tm=1024):
    b, s, d = x.shape
    ff = wd.shape[1]
    tok = lambda i, j, e: (i, j, 0)
    return pl.pallas_call(
        _moe_dense_kernel,
        grid=(b, s // tm, N_EXPERTS),
        in_specs=[
            pl.BlockSpec((1, tm, d), tok),
            pl.BlockSpec((1, tm, N_EXPERTS), tok),
            pl.BlockSpec((1, tm, d), tok),
            pl.BlockSpec((1, 6, d), lambda i, j, e: (i, 0, 0)),
            pl.BlockSpec((1, d, 2 * ff), lambda i, j, e: (e, 0, 0)),
            pl.BlockSpec((1, ff, d), lambda i, j, e: (e, 0, 0)),
        ],
        out_specs=pl.BlockSpec((1, tm, d), tok),
        out_shape=jax.ShapeDtypeStruct((b, s, d), F32),
        scratch_shapes=[pltpu.VMEM((tm, d), F32)],
        compiler_params=_params("arbitrary", "arbitrary", "arbitrary"),
        name="moe_ffn",
    )(h, gates, x, mod, wgu, wd)


def _moe_layer(h2, gates_rows, x1, mod, w_gate, w_up, w_down):
    b, s, _ = x1.shape
    gates = gates_rows.T.reshape(b, s, N_EXPERTS)
    wgu = jnp.concatenate([w_gate, w_up], axis=-1).astype(BF16)
    return _moe_dense(h2, gates, x1, mod, wgu, w_down.astype(BF16))


def kernel(x, c, positions, ada_w, ada_b, norm1_g, norm2_g, mla_w_dqkv, mla_q_norm_g, mla_w_uq, mla_kv_norm_g, mla_w_ukv, mla_qk_q_g, mla_qk_k_g, mla_w_o, conv_pw1_w, conv_pw1_b, conv_dw_w, conv_dw_b, conv_ln_g, conv_ln_b, conv_pw2_w, conv_pw2_b, router_w, router_bias, moe_w_gate, moe_w_up, moe_w_down):
    b, s, d = x.shape
    depth = ada_w.shape[0]
    mod_all = _adaln(c, ada_w, ada_b).reshape(depth, b, 6, d)

    rw_t = router_w.astype(F32).T
    rw_hi = rw_t.astype(BF16)
    rw_lo = (rw_t - rw_hi.astype(F32)).astype(BF16)
    rw = jnp.concatenate([rw_hi, rw_lo], axis=0)
    rb = router_bias.astype(F32).reshape(N_EXPERTS, 1)

    for i in range(depth):
        mod = mod_all[i]
        jdx = i // 2
        if i % 2 == 0:
            w_uq = mla_w_uq[jdx].reshape(Q_LORA, N_HEADS, QK_HEAD)
            wuq = jnp.concatenate([w_uq[:, :, :QK_NOPE].reshape(Q_LORA, -1),
                                   w_uq[:, :, QK_NOPE:].reshape(Q_LORA, -1)], axis=-1).astype(BF16)
            w_ukv = mla_w_ukv[jdx].reshape(KV_LORA, N_HEADS, QK_NOPE + V_HEAD)
            wukv = jnp.concatenate([w_ukv[:, :, :QK_NOPE].reshape(KV_LORA, -1),
                                    w_ukv[:, :, QK_NOPE:].reshape(KV_LORA, -1)], axis=-1).astype(BF16)
            q, k, v = _mla_proj(x, mod, norm1_g[i], mla_w_dqkv[jdx].astype(BF16), mla_q_norm_g[jdx], wuq,
                                mla_kv_norm_g[jdx], wukv, mla_qk_q_g[jdx], mla_qk_k_g[jdx], positions)
            o = _flash_attention(q, k, v)
            x1, h2, gates_rows = _attn_out(o, x, mod, mla_w_o[jdx].astype(BF16), norm2_g[i], rw, rb)
        else:
            x1, h2, gates_rows = _conv_mixer(
                x, mod, norm1_g[i], conv_pw1_w[jdx].astype(BF16), conv_pw1_b[jdx], conv_dw_w[jdx],
                conv_dw_b[jdx], conv_ln_g[jdx], conv_ln_b[jdx], conv_pw2_w[jdx].astype(BF16),
                conv_pw2_b[jdx], norm2_g[i], rw, rb)
        x = _moe_layer(h2, gates_rows, x1, mod, moe_w_gate[i], moe_w_up[i], moe_w_down[i])
    return x
```

```python
import functools

import jax
import jax.numpy as jnp
from jax import lax
from jax.experimental import pallas as pl
from jax.experimental.pallas import tpu as pltpu

F32 = jnp.float32
BF16 = jnp.bfloat16

N_HEADS = 8
QK_NOPE = 128
QK_ROPE = 64
QK_HEAD = QK_NOPE + QK_ROPE
V_HEAD = 128
Q_LORA = 768
KV_LORA = 256
ROPE_THETA = 10000.0
CONV_WIDTH = 31
CONV_HALO = 32
N_EXPERTS = 16
N_GROUPS = 4
EXPERTS_PER_GROUP = N_EXPERTS // N_GROUPS
TOP_K = 2
ROUTE_ROWS = 8
NORM_EPS = 1e-6
LOG2_E = 1.4426950408889634
LANES = 128
SUBLANES = 8
ATTN_TILE = 512
VMEM_LIMIT_BYTES = 52 * 1024 * 1024


def _params(*semantics):
    return pltpu.CompilerParams(dimension_semantics=semantics, vmem_limit_bytes=VMEM_LIMIT_BYTES)


def _rms(x):
    return x * lax.rsqrt(jnp.mean(x * x, axis=-1, keepdims=True) + NORM_EPS)


def _silu(x):
    return x * jax.nn.sigmoid(x)


def _modulated_norm(x, g, scale, shift):
    return (_rms(x) * g) * (1.0 + scale) + shift


def _adaln_kernel(c_ref, w_ref, b_ref, o_ref):
    a = _silu(c_ref[...]).astype(BF16)
    o_ref[0] = jnp.dot(a, w_ref[0].astype(BF16), preferred_element_type=F32) + b_ref[0]


def _adaln(c, ada_w, ada_b, tn=1536):
    depth, d, n = ada_w.shape
    b = c.shape[0]
    return pl.pallas_call(
        _adaln_kernel,
        grid=(depth, n // tn),
        in_specs=[
            pl.BlockSpec((b, d), lambda l, j: (0, 0)),
            pl.BlockSpec((1, d, tn), lambda l, j: (l, 0, j)),
            pl.BlockSpec((1, 1, tn), lambda l, j: (l, 0, j)),
        ],
        out_specs=pl.BlockSpec((1, b, tn), lambda l, j: (l, 0, j)),
        out_shape=jax.ShapeDtypeStruct((depth, b, n), F32),
        compiler_params=_params("arbitrary", "arbitrary"),
        name="adaln",
    )(c, ada_w, ada_b.reshape(depth, 1, n))


def _rope128(x, cos, sin_signed, lo_mask):
    swapped = jnp.where(lo_mask, pltpu.roll(x, LANES - QK_ROPE // 2, 1), pltpu.roll(x, QK_ROPE // 2, 1))
    return x * cos + swapped * sin_signed


def _mla_proj_kernel(x_ref, mod_ref, n1g_ref, wd_ref, qg_ref, wuq_ref, kvg_ref, wuk_ref, wuvt_ref,
                     gq_ref, gk_ref, pos_ref, invf_ref, q_out, k_out, vt_out):
    x = x_ref[0]
    h = _modulated_norm(x, n1g_ref[...], mod_ref[0, 1:2, :], mod_ref[0, 0:1, :]).astype(BF16)
    lat = jnp.dot(h, wd_ref[...], preferred_element_type=F32)
    c_q = (_rms(lat[:, :Q_LORA]) * qg_ref[...]).astype(BF16)
    c_kv = (_rms(lat[:, Q_LORA:Q_LORA + KV_LORA]) * kvg_ref[...]).astype(BF16)
    k_pe = lat[:, Q_LORA + KV_LORA:]
    q = jnp.dot(c_q, wuq_ref[...], preferred_element_type=F32)
    k_nope = jnp.dot(c_kv, wuk_ref[...], preferred_element_type=F32)
    v_t = lax.dot_general(wuvt_ref[...], c_kv, (((1,), (1,)), ((), ())), preferred_element_type=F32)

    ang = pos_ref[0].astype(F32) * invf_ref[...]
    cos = jnp.cos(ang)
    lane = lax.broadcasted_iota(jnp.int32, (1, LANES), 1)
    lo_mask = (lane % QK_ROPE) < (QK_ROPE // 2)
    sin_signed = jnp.where(lo_mask, -1.0, 1.0) * jnp.sin(ang)

    gq = gq_ref[...]
    gk = gk_ref[...]
    gq_rope2 = jnp.concatenate([gq[:, QK_NOPE:], gq[:, QK_NOPE:]], axis=-1)
    gk_rope2 = jnp.concatenate([gk[:, QK_NOPE:], gk[:, QK_NOPE:]], axis=-1)
    sm_scale = QK_HEAD ** -0.5 * LOG2_E
    nope_all = N_HEADS * QK_NOPE

    k_pe2 = jnp.concatenate([k_pe, k_pe], axis=-1)
    k_rope = _rope128(k_pe2 * gk_rope2, cos, sin_signed, lo_mask)[:, :QK_ROPE]
    ss_pe = jnp.sum(k_pe * k_pe, axis=-1, keepdims=True)

    for pair in range(N_HEADS // 2):
        q_rope_pair = q[:, nope_all + pair * LANES: nope_all + (pair + 1) * LANES]
        q_roped_pair = _rope128(q_rope_pair * gq_rope2, cos, sin_signed, lo_mask)
        for sub in range(2):
            hd = 2 * pair + sub
            qn = q[:, hd * QK_NOPE:(hd + 1) * QK_NOPE]
            qr = q_rope_pair[:, sub * QK_ROPE:(sub + 1) * QK_ROPE]
            ss = jnp.sum(qn * qn, axis=-1, keepdims=True) + jnp.sum(qr * qr, axis=-1, keepdims=True)
            r = lax.rsqrt(ss * (1.0 / QK_HEAD) + NORM_EPS) * sm_scale
            q_out[0, hd, :, 0:QK_NOPE] = (qn * r * gq[:, :QK_NOPE]).astype(BF16)
            q_out[0, hd, :, QK_NOPE:QK_HEAD] = (
                q_roped_pair[:, sub * QK_ROPE:(sub + 1) * QK_ROPE] * r).astype(BF16)

            kn = k_nope[:, hd * QK_NOPE:(hd + 1) * QK_NOPE]
            rk = lax.rsqrt((jnp.sum(kn * kn, axis=-1, keepdims=True) + ss_pe) * (1.0 / QK_HEAD) + NORM_EPS)
            k_out[0, hd, :, 0:QK_NOPE] = (kn * rk * gk[:, :QK_NOPE]).astype(BF16)
            k_out[0, hd, :, QK_NOPE:QK_HEAD] = (k_rope * rk).astype(BF16)
            vt_out[0, hd, 0] = v_t[hd * V_HEAD:(hd + 1) * V_HEAD, :].astype(BF16)


def _mla_proj(x, mod, n1g, wd, qg, wuq, kvg, wuk, wuvt, gq, gk, positions, ts):
    b, s, d = x.shape
    half = QK_ROPE // 2
    inv_freq = 1.0 / (ROPE_THETA ** (jnp.arange(half, dtype=F32) * (2.0 / QK_ROPE)))
    invf = jnp.tile(inv_freq, LANES // half).reshape(1, LANES)
    const = lambda i, j: (0, 0)
    return pl.pallas_call(
        _mla_proj_kernel,
        grid=(b, s // ts),
        in_specs=[
            pl.BlockSpec((1, ts, d), lambda i, j: (i, j, 0)),
            pl.BlockSpec((1, 6, d), lambda i, j: (i, 0, 0)),
            pl.BlockSpec((1, d), const),
            pl.BlockSpec(wd.shape, const),
            pl.BlockSpec((1, Q_LORA), const),
            pl.BlockSpec(wuq.shape, const),
            pl.BlockSpec((1, KV_LORA), const),
            pl.BlockSpec(wuk.shape, const),
            pl.BlockSpec(wuvt.shape, const),
            pl.BlockSpec((1, QK_HEAD), const),
            pl.BlockSpec((1, QK_HEAD), const),
            pl.BlockSpec((1, ts, 1), lambda i, j: (i, j, 0)),
            pl.BlockSpec((1, LANES), const),
        ],
        out_specs=[
            pl.BlockSpec((1, N_HEADS, ts, QK_HEAD), lambda i, j: (i, 0, j, 0)),
            pl.BlockSpec((1, N_HEADS, ts, QK_HEAD), lambda i, j: (i, 0, j, 0)),
            pl.BlockSpec((1, N_HEADS, 1, V_HEAD, ts), lambda i, j: (i, 0, j, 0, 0)),
        ],
        out_shape=[
            jax.ShapeDtypeStruct((b, N_HEADS, s, QK_HEAD), BF16),
            jax.ShapeDtypeStruct((b, N_HEADS, s, QK_HEAD), BF16),
            jax.ShapeDtypeStruct((b, N_HEADS, s // ts, V_HEAD, ts), BF16),
        ],
        compiler_params=_params("arbitrary", "arbitrary"),
        name="mla_proj",
    )(x, mod, n1g.reshape(1, d), wd, qg.reshape(1, -1), wuq, kvg.reshape(1, -1), wuk, wuvt,
      gq.reshape(1, -1), gk.reshape(1, -1), positions.reshape(b, s, 1), invf)


def _flash_kernel(q_ref, k_ref, vt_ref, o_ref, m_ref, l_ref, acc_ref, *stage_refs, t):
    qi = pl.program_id(2)
    heads = q_ref.shape[1]
    s_refs, mb_refs = stage_refs[:heads], stage_refs[heads:]
    nt = (((1,), (1,)), ((), ()))

    def scores(kb, hh, masked):
        start = pl.multiple_of(kb * t, t)
        s = lax.dot_general(k_ref[0, hh, pl.ds(start, t), :], q_ref[0, hh], nt,
                            preferred_element_type=F32)
        if masked:
            kv_pos = lax.broadcasted_iota(jnp.int32, (t, t), 0)
            q_pos = lax.broadcasted_iota(jnp.int32, (t, t), 1)
            s = jnp.where(kv_pos <= q_pos, s, -jnp.inf)
        s_refs[hh][kb % 2] = s
        mb_refs[hh][kb % 2] = jnp.max(s, axis=0, keepdims=True)

    def softmax_pv(kb, hh):
        m = m_ref[hh]
        m_new = jnp.maximum(m, mb_refs[hh][kb % 2])
        alpha = jnp.exp2(m - m_new)
        p = jnp.exp2(s_refs[hh][kb % 2] - m_new)
        l_ref[hh] = alpha * l_ref[hh] + jnp.sum(p, axis=0, keepdims=True)
        m_ref[hh] = m_new
        pv = jnp.dot(vt_ref[0, hh, kb], p.astype(BF16), preferred_element_type=F32)
        acc_ref[hh] = alpha * acc_ref[hh] + pv

    def stage(softmax_kb, scores_kb, masked):
        for hh in range(heads):
            if softmax_kb is not None:
                softmax_pv(softmax_kb, hh)
            if scores_kb is not None:
                scores(scores_kb, hh, masked)

    m_ref[...] = jnp.full(m_ref.shape, -jnp.inf, F32)
    l_ref[...] = jnp.zeros(l_ref.shape, F32)
    acc_ref[...] = jnp.zeros(acc_ref.shape, F32)

    @pl.when(qi == 0)
    def _():
        stage(None, qi, True)

    @pl.when(qi > 0)
    def _():
        stage(None, 0, False)

        def body(kb, carry):
            stage(kb, kb + 1, False)
            return carry

        lax.fori_loop(0, qi - 1, body, 0)
        stage(qi - 1, qi, True)

    stage(qi, None, False)
    for hh in range(heads):
        o_ref[0, :, hh * V_HEAD:(hh + 1) * V_HEAD] = (acc_ref[hh] * (1.0 / l_ref[hh])).T.astype(BF16)


def _flash_attention(q, k, vt, t, heads=2):
    b, h, s, dqk = q.shape
    dv = vt.shape[-2]
    assert vt.shape[-1] == t and h % heads == 0
    return pl.pallas_call(
        functools.partial(_flash_kernel, t=t),
        grid=(b, h // heads, s // t),
        in_specs=[
            pl.BlockSpec((1, heads, t, dqk), lambda i, j, n: (i, j, n, 0)),
            pl.BlockSpec((1, heads, s, dqk), lambda i, j, n: (i, j, 0, 0)),
            pl.BlockSpec((1, heads, s // t, dv, t), lambda i, j, n: (i, j, 0, 0, 0)),
        ],
        out_specs=pl.BlockSpec((1, t, heads * dv), lambda i, j, n: (i, n, j)),
        out_shape=jax.ShapeDtypeStruct((b, s, h * dv), BF16),
        compiler_params=_params("arbitrary", "arbitrary", "arbitrary"),
        scratch_shapes=(
            [pltpu.VMEM((heads, 1, t), F32),
             pltpu.VMEM((heads, 1, t), F32),
             pltpu.VMEM((heads, dv, t), F32)]
            + [pltpu.VMEM((2, t, t), F32)] * heads
            + [pltpu.VMEM((2, 1, t), F32)] * heads),
        name="flash_attention",
    )(q, k, vt)


def _route_rows(logits, bias_col):
    scores = jax.nn.sigmoid(logits)
    biased = scores + bias_col
    b_rows = [biased[i:i + 1, :] for i in range(N_EXPERTS)]
    s_rows = [scores[i:i + 1, :] for i in range(N_EXPERTS)]
    in_top = []
    group_score = []
    for g in range(N_GROUPS):
        grp = b_rows[g * EXPERTS_PER_GROUP:(g + 1) * EXPERTS_PER_GROUP]
        gs = None
        for i in range(EXPERTS_PER_GROUP):
            rank = None
            for j in range(EXPERTS_PER_GROUP):
                if j == i:
                    continue
                beats = (grp[j] >= grp[i]) if j < i else (grp[j] > grp[i])
                cnt = jnp.where(beats, 1.0, 0.0)
                rank = cnt if rank is None else rank + cnt
            sel = rank < float(TOP_K)
            in_top.append(sel)
            term = jnp.where(sel, grp[i], 0.0)
            gs = term if gs is None else gs + term
        group_score.append(gs)
    picked = []
    for g in range(N_GROUPS):
        lost = None
        for g2 in range(N_GROUPS):
            if g2 == g:
                continue
            beats = (group_score[g2] >= group_score[g]) if g2 < g else (group_score[g2] > group_score[g])
            cnt = jnp.where(beats, 1.0, 0.0)
            lost = cnt if lost is None else lost + cnt
        picked.append(lost < 1.0)
    e_lo = e_hi = w_lo = w_hi = denom = None
    for i in range(N_EXPERTS):
        sel = jnp.where(picked[i // EXPERTS_PER_GROUP], jnp.where(in_top[i], 1.0, 0.0), 0.0)
        w = sel * s_rows[i]
        if i == 0:
            seen = sel
            e_lo, e_hi, w_lo, w_hi, denom = jnp.zeros_like(w), jnp.zeros_like(w), w, jnp.zeros_like(w), w
        else:
            first = sel * (1.0 - seen)
            second = sel * seen
            e_lo = e_lo + first * float(i)
            e_hi = e_hi + second * float(i)
            w_lo = w_lo + first * s_rows[i]
            w_hi = w_hi + second * s_rows[i]
            denom = denom + w
            seen = jnp.maximum(seen, sel)
    inv = 1.0 / denom
    pad = jnp.zeros((ROUTE_ROWS - 4,) + e_lo.shape[1:], F32)
    return jnp.concatenate([e_lo, e_hi, w_lo * inv, w_hi * inv, pad], axis=0)


def _split_bf16(x):
    hi = x.astype(BF16)
    lo = (x - hi.astype(F32)).astype(BF16)
    return hi, lo


def _pack_bf16_pairs(x):
    half = x.shape[-1] // 2
    bits = lax.bitcast_convert_type(x.astype(BF16).astype(F32), jnp.uint32)
    return (bits[:, :half] >> 16) | (bits[:, half:] & jnp.uint32(0xFFFF0000))


def _unpack_bf16_pairs(words):
    lo = lax.bitcast_convert_type(words << 16, F32)
    hi = lax.bitcast_convert_type(words & jnp.uint32(0xFFFF0000), F32)
    return jnp.concatenate([lo, hi], axis=-1)


def _mixer_tail(x, y, mod_ref, n2g, rw_ref, rb_ref, x_out, h_out, g_out):
    x1 = x + mod_ref[0, 2:3, :] * y
    x_out[0] = x1
    h2 = _modulated_norm(x1, n2g, mod_ref[0, 4:5, :], mod_ref[0, 3:4, :])
    h_hi, h_lo = _split_bf16(h2)
    h_out[0] = _pack_bf16_pairs(h2)
    nt = (((1,), (1,)), ((), ()))
    a = lax.dot_general(rw_ref[...], h_hi, nt, preferred_element_type=F32)
    c = lax.dot_general(rw_ref[0:N_EXPERTS, :], h_lo, nt, preferred_element_type=F32)
    logits = a[0:N_EXPERTS, :] + a[N_EXPERTS:, :] + c
    g_out[...] = _route_rows(logits, rb_ref[...])


def _attn_out_kernel(o_ref, x_ref, mod_ref, wo_ref, n2g_ref, rw_ref, rb_ref, x_out, h_out, g_out):
    y = jnp.dot(o_ref[0], wo_ref[...], preferred_element_type=F32)
    _mixer_tail(x_ref[0], y, mod_ref, n2g_ref[...], rw_ref, rb_ref, x_out, h_out, g_out)


def _tail_out_specs(b, s, d, tm):
    nt = s // tm
    specs = [
        pl.BlockSpec((1, tm, d), lambda i, j: (i, j, 0)),
        pl.BlockSpec((1, tm, d // 2), lambda i, j: (i, j, 0)),
        pl.BlockSpec((ROUTE_ROWS, tm), lambda i, j: (0, i * nt + j)),
    ]
    shapes = [
        jax.ShapeDtypeStruct((b, s, d), F32),
        jax.ShapeDtypeStruct((b, s, d // 2), jnp.uint32),
        jax.ShapeDtypeStruct((ROUTE_ROWS, b * s), F32),
    ]
    return specs, shapes


def _attn_out(o, x, mod, wo, n2g, rw, rb, tm=512):
    b, s, d = x.shape
    const = lambda i, j: (0, 0)
    out_specs, out_shape = _tail_out_specs(b, s, d, tm)
    return pl.pallas_call(
        _attn_out_kernel,
        grid=(b, s // tm),
        in_specs=[
            pl.BlockSpec((1, tm, o.shape[-1]), lambda i, j: (i, j, 0)),
            pl.BlockSpec((1, tm, d), lambda i, j: (i, j, 0)),
            pl.BlockSpec((1, 6, d), lambda i, j: (i, 0, 0)),
            pl.BlockSpec(wo.shape, const),
            pl.BlockSpec((1, d), const),
            pl.BlockSpec(rw.shape, const),
            pl.BlockSpec((N_EXPERTS, 1), const),
        ],
        out_specs=out_specs,
        out_shape=out_shape,
        compiler_params=_params("arbitrary", "arbitrary"),
        name="attn_out_router",
    )(o, x, mod, wo, n2g.reshape(1, d), rw, rb)


def _conv_mixer_kernel(x_ref, mod_ref, n1g_ref, w1_ref, b1_ref, dww_ref, dwb_ref, lng_ref, lnb_ref,
                       w2_ref, b2_ref, n2g_ref, rw_ref, rb_ref, x_out, h_out, g_out, ubuf, vbuf, cbuf,
                       *, ts, rows):
    j = pl.program_id(1)
    x = x_ref[0]
    d = x.shape[-1]
    h = _modulated_norm(x, n1g_ref[...], mod_ref[0, 1:2, :], mod_ref[0, 0:1, :]).astype(BF16)
    u = jnp.dot(h, w1_ref[...], preferred_element_type=F32) + b1_ref[...]
    u = u[:, :d] * jax.nn.sigmoid(u[:, d:])

    nc = d // LANES

    @pl.when(j == 0)
    def _():
        ubuf[:, 0:CONV_HALO, :] = jnp.zeros((nc, CONV_HALO, LANES), F32)

    @pl.when(j > 0)
    def _():
        ubuf[:, 0:CONV_HALO, :] = ubuf[:, ts:ts + CONV_HALO, :]

    for cc in range(nc):
        ubuf[cc, CONV_HALO:CONV_HALO + ts, :] = u[:, cc * LANES:(cc + 1) * LANES]

    first = CONV_HALO - (CONV_WIDTH - 1)

    def conv_chunk(cc, carry):
        for r in range(8):
            span = ts + 8 * ((CONV_WIDTH - 1 - r) // 8)
            vbuf[r, 0:span, :] = ubuf[cc, first + r:first + r + span, :]
        for i in range(ts // rows):
            acc = None
            for tap in range(CONV_WIDTH):
                r0 = i * rows + 8 * (tap // 8)
                term = dww_ref[cc, tap:tap + 1, :] * vbuf[tap % 8, r0:r0 + rows, :]
                acc = term if acc is None else acc + term
            cbuf[cc, i * rows:(i + 1) * rows, :] = acc
        return carry

    lax.fori_loop(0, nc, conv_chunk, 0)
    v = jnp.concatenate([cbuf[cc] for cc in range(nc)], axis=-1) + dwb_ref[...]
    mu = jnp.mean(v, axis=-1, keepdims=True)
    vc = v - mu
    var = jnp.mean(vc * vc, axis=-1, keepdims=True)
    v = _silu(vc * lax.rsqrt(var + NORM_EPS) * lng_ref[...] + lnb_ref[...])
    y = jnp.dot(v.astype(BF16), w2_ref[...], preferred_element_type=F32) + b2_ref[...]
    _mixer_tail(x, y, mod_ref, n2g_ref[...], rw_ref, rb_ref, x_out, h_out, g_out)


def _conv_mixer(x, mod, n1g, w1, b1, dww, dwb, lng, lnb, w2, b2, n2g, rw, rb, ts=512, rows=64):
    b, s, d = x.shape
    nc = d // LANES
    const = lambda i, j: (0, 0)
    row = lambda a: a.reshape(1, -1)
    dww = dww.reshape(CONV_WIDTH, nc, LANES).transpose(1, 0, 2)
    out_specs, out_shape = _tail_out_specs(b, s, d, ts)
    return pl.pallas_call(
        functools.partial(_conv_mixer_kernel, ts=ts, rows=rows),
        grid=(b, s // ts),
        in_specs=[
            pl.BlockSpec((1, ts, d), lambda i, j: (i, j, 0)),
            pl.BlockSpec((1, 6, d), lambda i, j: (i, 0, 0)),
            pl.BlockSpec((1, d), const),
            pl.BlockSpec(w1.shape, const),
            pl.BlockSpec((1, 2 * d), const),
            pl.BlockSpec(dww.shape, lambda i, j: (0, 0, 0)),
            pl.BlockSpec((1, d), const),
            pl.BlockSpec((1, d), const),
            pl.BlockSpec((1, d), const),
            pl.BlockSpec(w2.shape, const),
            pl.BlockSpec((1, d), const),
            pl.BlockSpec((1, d), const),
            pl.BlockSpec(rw.shape, const),
            pl.BlockSpec((N_EXPERTS, 1), const),
        ],
        out_specs=out_specs,
        out_shape=out_shape,
        scratch_shapes=[pltpu.VMEM((nc, ts + CONV_HALO, LANES), F32),
                        pltpu.VMEM((8, ts + CONV_HALO, LANES), F32),
                        pltpu.VMEM((nc, ts, LANES), F32)],
        compiler_params=_params("arbitrary", "arbitrary"),
        name="conv_mixer_router",
    )(x, mod, row(n1g), w1, row(b1), dww, row(dwb), row(lng), row(lnb), w2, row(b2), row(n2g), rw, rb)


MOE_TILE = 512


def _dispatch_plan(route, tm):
    t = route.shape[1]
    n = TOP_K * t
    n_tiles_max = n // tm + N_EXPERTS
    i32 = jnp.int32
    e = route[0:TOP_K].astype(i32)
    w = route[TOP_K:2 * TOP_K]
    onehot = (e[:, :, None] == jnp.arange(N_EXPERTS, dtype=i32)).astype(i32)
    csum = jnp.cumsum(onehot, axis=1)
    cnt_slot = csum[:, -1, :]
    cnt = cnt_slot[0] + cnt_slot[1]
    rank = jnp.take_along_axis(csum - onehot, e[:, :, None], axis=2)[:, :, 0]
    rank = rank + jnp.stack([jnp.zeros((t,), i32), cnt_slot[0][e[1]]])
    padded = (cnt + tm - 1) // tm * tm
    pad_end = jnp.cumsum(padded)
    off = pad_end - padded
    pos = off[e] + rank
    n_tiles = (pad_end[-1] // tm).astype(i32)
    tile = jnp.minimum(jnp.arange(n_tiles_max, dtype=i32), n_tiles - 1)
    tile_expert = jnp.searchsorted(pad_end // tm, tile, side="right").astype(i32)
    order = jnp.argsort(e.reshape(n), stable=True).astype(i32)
    start = jnp.cumsum(cnt) - cnt
    p = jnp.arange(n_tiles_max * tm, dtype=i32)
    te = tile_expert[p // tm]
    r = p - off[te]
    valid = (r < cnt[te]) & (p // tm < n_tiles)
    a = order[jnp.clip(start[te] + r, 0, n - 1)]
    src_token = jnp.where(valid, a % t, 0)
    w_sorted = jnp.where(valid, w.reshape(n)[a], 0.0)
    return dict(src_token=src_token.reshape(n_tiles_max, 1, tm), w_sorted=w_sorted.reshape(n_tiles_max, tm, 1),
                tile_expert=tile_expert, n_tiles=n_tiles.reshape(1), pos=pos)


def _gather_pipeline_step(step, n_steps, idx_hbm, src_hbm, idx_smem, buf, isem, gsem, rows):
    n_steps = jnp.asarray(n_steps, jnp.int32)

    def idx_copy(s, sl):
        return pltpu.make_async_copy(idx_hbm.at[s], idx_smem.at[sl], isem.at[sl])

    def start_rows(sl):
        def body(g, carry):
            base = pl.multiple_of(g * SUBLANES, SUBLANES)
            for j in range(SUBLANES):
                row = idx_smem[sl, 0, base + j]
                pltpu.make_async_copy(src_hbm.at[row >> 3, pl.ds(row & (SUBLANES - 1), 1)],
                                      buf.at[sl, g, pl.ds(j, 1)], gsem.at[sl]).start()
            return carry

        lax.fori_loop(0, rows // SUBLANES, body, 0)

    def pipeline(slot):
        nslot = 1 - slot

        @pl.when(step == 0)
        def _():
            first = idx_copy(0, 0)
            first.start()
            first.wait()
            start_rows(0)

            @pl.when(n_steps > 1)
            def _():
                idx_copy(1, 1).start()

        @pl.when(step + 1 < n_steps)
        def _():
            idx_copy(step + 1, nslot).wait()
            start_rows(nslot)

            @pl.when(step + 2 < n_steps)
            def _():
                idx_copy(step + 2, slot).start()

        @pl.when(step < n_steps)
        def _():
            pltpu.make_async_copy(src_hbm.at[pl.ds(0, rows // SUBLANES)], buf.at[slot], gsem.at[slot]).wait()

    for parity in range(2):
        pl.when(step % 2 == parity)(functools.partial(pipeline, parity))


def _gather_scratch(rows, words):
    return [pltpu.SMEM((2, 1, rows), jnp.int32), pltpu.VMEM((2, rows // SUBLANES, SUBLANES, words), jnp.uint32),
            pltpu.SemaphoreType.DMA((2,)), pltpu.SemaphoreType.DMA((2,))]


def _moe_expert_kernel(te_ref, nt_ref, idx_hbm, h_hbm, w_ref, wgu_ref, wd_ref, y_ref,
                       idx_smem, xbuf, isem, gsem, *, tm):
    i = pl.program_id(0)
    n = nt_ref[0]
    ff = wd_ref.shape[1]
    _gather_pipeline_step(i, n, idx_hbm, h_hbm, idx_smem, xbuf, isem, gsem, tm)

    @pl.when(i < n)
    def _():
        x = _unpack_bf16_pairs(xbuf[i % 2].reshape(tm, -1)).astype(BF16)
        gu = jnp.dot(x, wgu_ref[0], preferred_element_type=F32)
        a = _silu(gu[:, :ff]) * gu[:, ff:] * w_ref[0]
        y = jnp.dot(a.astype(BF16), wd_ref[0], preferred_element_type=F32)
        y_ref[...] = _pack_bf16_pairs(y)

    @pl.when(i >= n)
    def _():
        y_ref[...] = jnp.zeros(y_ref.shape, jnp.uint32)


def _moe_experts(plan, h_words, wgu, wd, tm):
    n_tiles_max = plan["tile_expert"].shape[0]
    words = h_words.shape[1]
    d, ff = wgu.shape[1], wd.shape[1]
    grid_spec = pltpu.PrefetchScalarGridSpec(
        num_scalar_prefetch=2,
        grid=(n_tiles_max,),
        in_specs=[
            pl.BlockSpec(memory_space=pl.ANY),
            pl.BlockSpec(memory_space=pl.ANY),
            pl.BlockSpec((1, tm, 1), lambda i, te, nt: (i, 0, 0)),
            pl.BlockSpec((1, d, 2 * ff), lambda i, te, nt: (te[i], 0, 0)),
            pl.BlockSpec((1, ff, d), lambda i, te, nt: (te[i], 0, 0)),
        ],
        out_specs=pl.BlockSpec((tm, words), lambda i, te, nt: (i, 0)),
        scratch_shapes=_gather_scratch(tm, words),
    )
    return pl.pallas_call(
        functools.partial(_moe_expert_kernel, tm=tm),
        grid_spec=grid_spec,
        out_shape=jax.ShapeDtypeStruct((n_tiles_max * tm, words), jnp.uint32),
        compiler_params=_params("arbitrary"),
        name="moe_experts",
    )(plan["tile_expert"], plan["n_tiles"], plan["src_token"], h_words.reshape(-1, SUBLANES, words),
      plan["w_sorted"], wgu, wd)


def _moe_combine_kernel(pos_hbm, y_hbm, x_ref, mod_ref, o_ref, idx_smem, ybuf, isem, gsem, *, tm, n_steps):
    i = pl.program_id(0) * pl.num_programs(1) + pl.program_id(1)
    _gather_pipeline_step(i, n_steps, pos_hbm, y_hbm, idx_smem, ybuf, isem, gsem, TOP_K * tm)
    rows = _unpack_bf16_pairs(ybuf[i % 2].reshape(TOP_K * tm, -1))
    o_ref[0] = x_ref[0] + mod_ref[0, 5:6, :] * (rows[:tm] + rows[tm:])


def _moe_combine(pos, y_words, x, mod, tm):
    b, s, d = x.shape
    n_steps = b * s // tm
    words = y_words.shape[1]
    pos_tiles = pos.reshape(TOP_K, n_steps, tm).transpose(1, 0, 2).reshape(n_steps, 1, TOP_K * tm)
    return pl.pallas_call(
        functools.partial(_moe_combine_kernel, tm=tm, n_steps=n_steps),
        grid=(b, s // tm),
        in_specs=[
            pl.BlockSpec(memory_space=pl.ANY),
            pl.BlockSpec(memory_space=pl.ANY),
            pl.BlockSpec((1, tm, d), lambda i, j: (i, j, 0)),
            pl.BlockSpec((1, 6, d), lambda i, j: (i, 0, 0)),
        ],
        out_specs=pl.BlockSpec((1, tm, d), lambda i, j: (i, j, 0)),
        out_shape=jax.ShapeDtypeStruct((b, s, d), F32),
        scratch_shapes=_gather_scratch(TOP_K * tm, words),
        compiler_params=_params("arbitrary", "arbitrary"),
        name="moe_combine",
    )(pos_tiles, y_words.reshape(-1, SUBLANES, words), x, mod)


def _moe_layer(h_words, route, x1, mod, w_gate, w_up, w_down):
    b, s, d = x1.shape
    plan = _dispatch_plan(route, MOE_TILE)
    wgu = jnp.concatenate([w_gate, w_up], axis=-1).astype(BF16)
    y_words = _moe_experts(plan, h_words.reshape(b * s, d // 2), wgu, w_down.astype(BF16), MOE_TILE)
    return _moe_combine(plan["pos"], y_words, x1, mod, MOE_TILE)


def kernel(x, c, positions, ada_w, ada_b, norm1_g, norm2_g, mla_w_dqkv, mla_q_norm_g, mla_w_uq, mla_kv_norm_g, mla_w_ukv, mla_qk_q_g, mla_qk_k_g, mla_w_o, conv_pw1_w, conv_pw1_b, conv_dw_w, conv_dw_b, conv_ln_g, conv_ln_b, conv_pw2_w, conv_pw2_b, router_w, router_bias, moe_w_gate, moe_w_up, moe_w_down):
    b, s, d = x.shape
    depth = ada_w.shape[0]
    mod_all = _adaln(c, ada_w, ada_b).reshape(depth, b, 6, d)

    rw_t = router_w.astype(F32).T
    rw_hi = rw_t.astype(BF16)
    rw_lo = (rw_t - rw_hi.astype(F32)).astype(BF16)
    rw = jnp.concatenate([rw_hi, rw_lo], axis=0)
    rb = router_bias.astype(F32).reshape(N_EXPERTS, 1)

    for i in range(depth):
        mod = mod_all[i]
        jdx = i // 2
        if i % 2 == 0:
            w_uq = mla_w_uq[jdx].reshape(Q_LORA, N_HEADS, QK_HEAD)
            wuq = jnp.concatenate([w_uq[:, :, :QK_NOPE].reshape(Q_LORA, -1),
                                   w_uq[:, :, QK_NOPE:].reshape(Q_LORA, -1)], axis=-1).astype(BF16)
            w_ukv = mla_w_ukv[jdx].reshape(KV_LORA, N_HEADS, QK_NOPE + V_HEAD)
            wuk = w_ukv[:, :, :QK_NOPE].reshape(KV_LORA, -1).astype(BF16)
            wuvt = w_ukv[:, :, QK_NOPE:].reshape(KV_LORA, -1).T.astype(BF16)
            q, k, vt = _mla_proj(x, mod, norm1_g[i], mla_w_dqkv[jdx].astype(BF16), mla_q_norm_g[jdx], wuq,
                                 mla_kv_norm_g[jdx], wuk, wuvt, mla_qk_q_g[jdx], mla_qk_k_g[jdx], positions,
                                 ts=ATTN_TILE)
            o = _flash_attention(q, k, vt, t=ATTN_TILE)
            x1, h2, gates_rows = _attn_out(o, x, mod, mla_w_o[jdx].astype(BF16), norm2_g[i], rw, rb)
        else:
            x1, h2, gates_rows = _conv_mixer(
                x, mod, norm1_g[i], conv_pw1_w[jdx].astype(BF16), conv_pw1_b[jdx], conv_dw_w[jdx],
                conv_dw_b[jdx], conv_ln_g[jdx], conv_ln_b[jdx], conv_pw2_w[jdx].astype(BF16),
                conv_pw2_b[jdx], norm2_g[i], rw, rb)
        x = _moe_layer(h2, gates_rows, x1, mod, moe_w_gate[i], moe_w_up[i], moe_w_down[i])
    return x
```

```python
import functools

import jax
import jax.numpy as jnp
from jax import lax
from jax.experimental import pallas as pl
from jax.experimental.pallas import tpu as pltpu

F32 = jnp.float32
BF16 = jnp.bfloat16

N_HEADS = 8
QK_NOPE = 128
QK_ROPE = 64
QK_HEAD = QK_NOPE + QK_ROPE
V_HEAD = 128
Q_LORA = 768
KV_LORA = 256
ROPE_THETA = 10000.0
CONV_WIDTH = 31
CONV_HALO = 32
N_EXPERTS = 16
N_GROUPS = 4
EXPERTS_PER_GROUP = N_EXPERTS // N_GROUPS
TOP_K = 2
ROUTE_ROWS = 8
NORM_EPS = 1e-6
LOG2_E = 1.4426950408889634
LANES = 128
SUBLANES = 8
ATTN_TILE = 512
VMEM_LIMIT_BYTES = 52 * 1024 * 1024


def _params(*semantics):
    return pltpu.CompilerParams(dimension_semantics=semantics, vmem_limit_bytes=VMEM_LIMIT_BYTES)


def _rms(x):
    return x * lax.rsqrt(jnp.mean(x * x, axis=-1, keepdims=True) + NORM_EPS)


def _silu(x):
    return x * jax.nn.sigmoid(x)


def _modulated_norm(x, g, scale, shift):
    return (_rms(x) * g) * (1.0 + scale) + shift


def _adaln_kernel(c_ref, w_ref, b_ref, o_ref):
    a = _silu(c_ref[...]).astype(BF16)
    o_ref[0] = jnp.dot(a, w_ref[0].astype(BF16), preferred_element_type=F32) + b_ref[0]


def _adaln(c, ada_w, ada_b, tn=1536):
    depth, d, n = ada_w.shape
    b = c.shape[0]
    return pl.pallas_call(
        _adaln_kernel,
        grid=(depth, n // tn),
        in_specs=[
            pl.BlockSpec((b, d), lambda l, j: (0, 0)),
            pl.BlockSpec((1, d, tn), lambda l, j: (l, 0, j)),
            pl.BlockSpec((1, 1, tn), lambda l, j: (l, 0, j)),
        ],
        out_specs=pl.BlockSpec((1, b, tn), lambda l, j: (l, 0, j)),
        out_shape=jax.ShapeDtypeStruct((depth, b, n), F32),
        compiler_params=_params("arbitrary", "arbitrary"),
        name="adaln",
    )(c, ada_w, ada_b.reshape(depth, 1, n))


def _rope128(x, cos, sin_signed, lo_mask):
    swapped = jnp.where(lo_mask, pltpu.roll(x, LANES - QK_ROPE // 2, 1), pltpu.roll(x, QK_ROPE // 2, 1))
    return x * cos + swapped * sin_signed


def _mla_proj_kernel(x_ref, mod_ref, n1g_ref, wd_ref, qg_ref, wuq_ref, kvg_ref, wuk_ref, wuvt_ref,
                     gq_ref, gk_ref, pos_ref, invf_ref, q_out, k_out, vt_out):
    x = x_ref[0]
    h = _modulated_norm(x, n1g_ref[...], mod_ref[0, 1:2, :], mod_ref[0, 0:1, :]).astype(BF16)
    lat = jnp.dot(h, wd_ref[...], preferred_element_type=F32)
    c_q = (_rms(lat[:, :Q_LORA]) * qg_ref[...]).astype(BF16)
    c_kv = (_rms(lat[:, Q_LORA:Q_LORA + KV_LORA]) * kvg_ref[...]).astype(BF16)
    k_pe = lat[:, Q_LORA + KV_LORA:]
    q = jnp.dot(c_q, wuq_ref[...], preferred_element_type=F32)
    k_nope = jnp.dot(c_kv, wuk_ref[...], preferred_element_type=F32)
    v_t = lax.dot_general(wuvt_ref[...], c_kv, (((1,), (1,)), ((), ())), preferred_element_type=F32)

    ang = pos_ref[0].astype(F32) * invf_ref[...]
    cos = jnp.cos(ang)
    lane = lax.broadcasted_iota(jnp.int32, (1, LANES), 1)
    lo_mask = (lane % QK_ROPE) < (QK_ROPE // 2)
    sin_signed = jnp.where(lo_mask, -1.0, 1.0) * jnp.sin(ang)

    gq = gq_ref[...]
    gk = gk_ref[...]
    gq_rope2 = jnp.concatenate([gq[:, QK_NOPE:], gq[:, QK_NOPE:]], axis=-1)
    gk_rope2 = jnp.concatenate([gk[:, QK_NOPE:], gk[:, QK_NOPE:]], axis=-1)
    sm_scale = QK_HEAD ** -0.5 * LOG2_E
    nope_all = N_HEADS * QK_NOPE

    k_pe2 = jnp.concatenate([k_pe, k_pe], axis=-1)
    k_rope = _rope128(k_pe2 * gk_rope2, cos, sin_signed, lo_mask)[:, :QK_ROPE]
    ss_pe = jnp.sum(k_pe * k_pe, axis=-1, keepdims=True)

    for pair in range(N_HEADS // 2):
        q_rope_pair = q[:, nope_all + pair * LANES: nope_all + (pair + 1) * LANES]
        q_roped_pair = _rope128(q_rope_pair * gq_rope2, cos, sin_signed, lo_mask)
        for sub in range(2):
            hd = 2 * pair + sub
            qn = q[:, hd * QK_NOPE:(hd + 1) * QK_NOPE]
            qr = q_rope_pair[:, sub * QK_ROPE:(sub + 1) * QK_ROPE]
            ss = jnp.sum(qn * qn, axis=-1, keepdims=True) + jnp.sum(qr * qr, axis=-1, keepdims=True)
            r = lax.rsqrt(ss * (1.0 / QK_HEAD) + NORM_EPS) * sm_scale
            q_out[0, hd, :, 0:QK_NOPE] = (qn * r * gq[:, :QK_NOPE]).astype(BF16)
            q_out[0, hd, :, QK_NOPE:QK_HEAD] = (
                q_roped_pair[:, sub * QK_ROPE:(sub + 1) * QK_ROPE] * r).astype(BF16)

            kn = k_nope[:, hd * QK_NOPE:(hd + 1) * QK_NOPE]
            rk = lax.rsqrt((jnp.sum(kn * kn, axis=-1, keepdims=True) + ss_pe) * (1.0 / QK_HEAD) + NORM_EPS)
            k_out[0, hd, :, 0:QK_NOPE] = (kn * rk * gk[:, :QK_NOPE]).astype(BF16)
            k_out[0, hd, :, QK_NOPE:QK_HEAD] = (k_rope * rk).astype(BF16)
            vt_out[0, hd, 0] = v_t[hd * V_HEAD:(hd + 1) * V_HEAD, :].astype(BF16)


def _mla_proj(x, mod, n1g, wd, qg, wuq, kvg, wuk, wuvt, gq, gk, positions, ts):
    b, s, d = x.shape
    half = QK_ROPE // 2
    inv_freq = 1.0 / (ROPE_THETA ** (jnp.arange(half, dtype=F32) * (2.0 / QK_ROPE)))
    invf = jnp.tile(inv_freq, LANES // half).reshape(1, LANES)
    const = lambda i, j: (0, 0)
    return pl.pallas_call(
        _mla_proj_kernel,
        grid=(b, s // ts),
        in_specs=[
            pl.BlockSpec((1, ts, d), lambda i, j: (i, j, 0)),
            pl.BlockSpec((1, 6, d), lambda i, j: (i, 0, 0)),
            pl.BlockSpec((1, d), const),
            pl.BlockSpec(wd.shape, const),
            pl.BlockSpec((1, Q_LORA), const),
            pl.BlockSpec(wuq.shape, const),
            pl.BlockSpec((1, KV_LORA), const),
            pl.BlockSpec(wuk.shape, const),
            pl.BlockSpec(wuvt.shape, const),
            pl.BlockSpec((1, QK_HEAD), const),
            pl.BlockSpec((1, QK_HEAD), const),
            pl.BlockSpec((1, ts, 1), lambda i, j: (i, j, 0)),
            pl.BlockSpec((1, LANES), const),
        ],
        out_specs=[
            pl.BlockSpec((1, N_HEADS, ts, QK_HEAD), lambda i, j: (i, 0, j, 0)),
            pl.BlockSpec((1, N_HEADS, ts, QK_HEAD), lambda i, j: (i, 0, j, 0)),
            pl.BlockSpec((1, N_HEADS, 1, V_HEAD, ts), lambda i, j: (i, 0, j, 0, 0)),
        ],
        out_shape=[
            jax.ShapeDtypeStruct((b, N_HEADS, s, QK_HEAD), BF16),
            jax.ShapeDtypeStruct((b, N_HEADS, s, QK_HEAD), BF16),
            jax.ShapeDtypeStruct((b, N_HEADS, s // ts, V_HEAD, ts), BF16),
        ],
        compiler_params=_params("arbitrary", "arbitrary"),
        name="mla_proj",
    )(x, mod, n1g.reshape(1, d), wd, qg.reshape(1, -1), wuq, kvg.reshape(1, -1), wuk, wuvt,
      gq.reshape(1, -1), gk.reshape(1, -1), positions.reshape(b, s, 1), invf)


def _flash_kernel(q_ref, k_ref, vt_ref, o_ref, m_ref, l_ref, acc_ref, *stage_refs, t):
    qi = pl.program_id(2)
    heads = q_ref.shape[1]
    s_refs, mb_refs = stage_refs[:heads], stage_refs[heads:]
    nt = (((1,), (1,)), ((), ()))

    def scores(kb, hh, masked):
        start = pl.multiple_of(kb * t, t)
        s = lax.dot_general(k_ref[0, hh, pl.ds(start, t), :], q_ref[0, hh], nt,
                            preferred_element_type=F32)
        if masked:
            kv_pos = lax.broadcasted_iota(jnp.int32, (t, t), 0)
            q_pos = lax.broadcasted_iota(jnp.int32, (t, t), 1)
            s = jnp.where(kv_pos <= q_pos, s, -jnp.inf)
        s_refs[hh][kb % 2] = s
        mb_refs[hh][kb % 2] = jnp.max(s, axis=0, keepdims=True)

    def softmax_pv(kb, hh):
        m = m_ref[hh]
        m_new = jnp.maximum(m, mb_refs[hh][kb % 2])
        alpha = jnp.exp2(m - m_new)
        p = jnp.exp2(s_refs[hh][kb % 2] - m_new)
        l_ref[hh] = alpha * l_ref[hh] + jnp.sum(p, axis=0, keepdims=True)
        m_ref[hh] = m_new
        pv = jnp.dot(vt_ref[0, hh, kb], p.astype(BF16), preferred_element_type=F32)
        acc_ref[hh] = alpha * acc_ref[hh] + pv

    def stage(softmax_kb, scores_kb, masked):
        for hh in range(heads):
            if softmax_kb is not None:
                softmax_pv(softmax_kb, hh)
            if scores_kb is not None:
                scores(scores_kb, hh, masked)

    m_ref[...] = jnp.full(m_ref.shape, -jnp.inf, F32)
    l_ref[...] = jnp.zeros(l_ref.shape, F32)
    acc_ref[...] = jnp.zeros(acc_ref.shape, F32)

    @pl.when(qi == 0)
    def _():
        stage(None, qi, True)

    @pl.when(qi > 0)
    def _():
        stage(None, 0, False)

        def body(kb, carry):
            stage(kb, kb + 1, False)
            return carry

        lax.fori_loop(0, qi - 1, body, 0)
        stage(qi - 1, qi, True)

    stage(qi, None, False)
    for hh in range(heads):
        o_ref[0, :, hh * V_HEAD:(hh + 1) * V_HEAD] = (acc_ref[hh] * (1.0 / l_ref[hh])).T.astype(BF16)


def _flash_attention(q, k, vt, t, heads=2):
    b, h, s, dqk = q.shape
    dv = vt.shape[-2]
    assert vt.shape[-1] == t and h % heads == 0
    return pl.pallas_call(
        functools.partial(_flash_kernel, t=t),
        grid=(b, h // heads, s // t),
        in_specs=[
            pl.BlockSpec((1, heads, t, dqk), lambda i, j, n: (i, j, n, 0)),
            pl.BlockSpec((1, heads, s, dqk), lambda i, j, n: (i, j, 0, 0)),
            pl.BlockSpec((1, heads, s // t, dv, t), lambda i, j, n: (i, j, 0, 0, 0)),
        ],
        out_specs=pl.BlockSpec((1, t, heads * dv), lambda i, j, n: (i, n, j)),
        out_shape=jax.ShapeDtypeStruct((b, s, h * dv), BF16),
        compiler_params=_params("arbitrary", "arbitrary", "arbitrary"),
        scratch_shapes=(
            [pltpu.VMEM((heads, 1, t), F32),
             pltpu.VMEM((heads, 1, t), F32),
             pltpu.VMEM((heads, dv, t), F32)]
            + [pltpu.VMEM((2, t, t), F32)] * heads
            + [pltpu.VMEM((2, 1, t), F32)] * heads),
        name="flash_attention",
    )(q, k, vt)


def _route_rows(logits, bias_col):
    scores = jax.nn.sigmoid(logits)
    biased = scores + bias_col
    b_rows = [biased[i:i + 1, :] for i in range(N_EXPERTS)]
    s_rows = [scores[i:i + 1, :] for i in range(N_EXPERTS)]
    in_top = []
    group_score = []
    for g in range(N_GROUPS):
        grp = b_rows[g * EXPERTS_PER_GROUP:(g + 1) * EXPERTS_PER_GROUP]
        gs = None
        for i in range(EXPERTS_PER_GROUP):
            rank = None
            for j in range(EXPERTS_PER_GROUP):
                if j == i:
                    continue
                beats = (grp[j] >= grp[i]) if j < i else (grp[j] > grp[i])
                cnt = jnp.where(beats, 1.0, 0.0)
                rank = cnt if rank is None else rank + cnt
            sel = rank < float(TOP_K)
            in_top.append(sel)
            term = jnp.where(sel, grp[i], 0.0)
            gs = term if gs is None else gs + term
        group_score.append(gs)
    picked = []
    for g in range(N_GROUPS):
        lost = None
        for g2 in range(N_GROUPS):
            if g2 == g:
                continue
            beats = (group_score[g2] >= group_score[g]) if g2 < g else (group_score[g2] > group_score[g])
            cnt = jnp.where(beats, 1.0, 0.0)
            lost = cnt if lost is None else lost + cnt
        picked.append(lost < 1.0)
    e_lo = e_hi = w_lo = w_hi = denom = None
    for i in range(N_EXPERTS):
        sel = jnp.where(picked[i // EXPERTS_PER_GROUP], jnp.where(in_top[i], 1.0, 0.0), 0.0)
        w = sel * s_rows[i]
        if i == 0:
            seen = sel
            e_lo, e_hi, w_lo, w_hi, denom = jnp.zeros_like(w), jnp.zeros_like(w), w, jnp.zeros_like(w), w
        else:
            first = sel * (1.0 - seen)
            second = sel * seen
            e_lo = e_lo + first * float(i)
            e_hi = e_hi + second * float(i)
            w_lo = w_lo + first * s_rows[i]
            w_hi = w_hi + second * s_rows[i]
            denom = denom + w
            seen = jnp.maximum(seen, sel)
    inv = 1.0 / denom
    pad = jnp.zeros((ROUTE_ROWS - 4,) + e_lo.shape[1:], F32)
    return jnp.concatenate([e_lo, e_hi, w_lo * inv, w_hi * inv, pad], axis=0)


def _split_bf16(x):
    hi = x.astype(BF16)
    lo = (x - hi.astype(F32)).astype(BF16)
    return hi, lo


def _pack_bf16_pairs(x):
    half = x.shape[-1] // 2
    bits = lax.bitcast_convert_type(x.astype(BF16).astype(F32), jnp.uint32)
    return (bits[:, :half] >> 16) | (bits[:, half:] & jnp.uint32(0xFFFF0000))


def _unpack_bf16_pairs(words):
    lo = lax.bitcast_convert_type(words << 16, F32)
    hi = lax.bitcast_convert_type(words & jnp.uint32(0xFFFF0000), F32)
    return jnp.concatenate([lo, hi], axis=-1)


def _mixer_tail(x, y, mod_ref, n2g, rw_ref, rb_ref, x_out, h_out, g_out):
    x1 = x + mod_ref[0, 2:3, :] * y
    x_out[0] = x1
    h2 = _modulated_norm(x1, n2g, mod_ref[0, 4:5, :], mod_ref[0, 3:4, :])
    h_hi, h_lo = _split_bf16(h2)
    half = h2.shape[-1] // 2
    h_out[0, :, 0:half] = _pack_bf16_pairs(h2)
    nt = (((1,), (1,)), ((), ()))
    a = lax.dot_general(rw_ref[...], h_hi, nt, preferred_element_type=F32)
    c = lax.dot_general(rw_ref[0:N_EXPERTS, :], h_lo, nt, preferred_element_type=F32)
    logits = a[0:N_EXPERTS, :] + a[N_EXPERTS:, :] + c
    route = _route_rows(logits, rb_ref[...])
    g_out[...] = route
    pad = jnp.zeros((LANES - ROUTE_ROWS, route.shape[1]), F32)
    meta = jnp.concatenate([route, pad], axis=0).T
    h_out[0, :, half:half + LANES] = lax.bitcast_convert_type(meta, jnp.uint32)


def _attn_out_kernel(o_ref, x_ref, mod_ref, wo_ref, n2g_ref, rw_ref, rb_ref, x_out, h_out, g_out):
    y = jnp.dot(o_ref[0], wo_ref[...], preferred_element_type=F32)
    _mixer_tail(x_ref[0], y, mod_ref, n2g_ref[...], rw_ref, rb_ref, x_out, h_out, g_out)


def _tail_out_specs(b, s, d, tm):
    nt = s // tm
    specs = [
        pl.BlockSpec((1, tm, d), lambda i, j: (i, j, 0)),
        pl.BlockSpec((1, tm, d // 2 + LANES), lambda i, j: (i, j, 0)),
        pl.BlockSpec((ROUTE_ROWS, tm), lambda i, j: (0, i * nt + j)),
    ]
    shapes = [
        jax.ShapeDtypeStruct((b, s, d), F32),
        jax.ShapeDtypeStruct((b, s, d // 2 + LANES), jnp.uint32),
        jax.ShapeDtypeStruct((ROUTE_ROWS, b * s), F32),
    ]
    return specs, shapes


def _attn_out(o, x, mod, wo, n2g, rw, rb, tm=512):
    b, s, d = x.shape
    const = lambda i, j: (0, 0)
    out_specs, out_shape = _tail_out_specs(b, s, d, tm)
    return pl.pallas_call(
        _attn_out_kernel,
        grid=(b, s // tm),
        in_specs=[
            pl.BlockSpec((1, tm, o.shape[-1]), lambda i, j: (i, j, 0)),
            pl.BlockSpec((1, tm, d), lambda i, j: (i, j, 0)),
            pl.BlockSpec((1, 6, d), lambda i, j: (i, 0, 0)),
            pl.BlockSpec(wo.shape, const),
            pl.BlockSpec((1, d), const),
            pl.BlockSpec(rw.shape, const),
            pl.BlockSpec((N_EXPERTS, 1), const),
        ],
        out_specs=out_specs,
        out_shape=out_shape,
        compiler_params=_params("arbitrary", "arbitrary"),
        name="attn_out_router",
    )(o, x, mod, wo, n2g.reshape(1, d), rw, rb)


def _conv_mixer_kernel(x_ref, mod_ref, n1g_ref, w1_ref, b1_ref, dww_ref, dwb_ref, lng_ref, lnb_ref,
                       w2_ref, b2_ref, n2g_ref, rw_ref, rb_ref, x_out, h_out, g_out, ubuf, vbuf, cbuf,
                       *, ts, rows):
    j = pl.program_id(1)
    x = x_ref[0]
    d = x.shape[-1]
    h = _modulated_norm(x, n1g_ref[...], mod_ref[0, 1:2, :], mod_ref[0, 0:1, :]).astype(BF16)
    u = jnp.dot(h, w1_ref[...], preferred_element_type=F32) + b1_ref[...]
    u = u[:, :d] * jax.nn.sigmoid(u[:, d:])

    nc = d // LANES

    @pl.when(j == 0)
    def _():
        ubuf[:, 0:CONV_HALO, :] = jnp.zeros((nc, CONV_HALO, LANES), F32)

    @pl.when(j > 0)
    def _():
        ubuf[:, 0:CONV_HALO, :] = ubuf[:, ts:ts + CONV_HALO, :]

    for cc in range(nc):
        ubuf[cc, CONV_HALO:CONV_HALO + ts, :] = u[:, cc * LANES:(cc + 1) * LANES]

    first = CONV_HALO - (CONV_WIDTH - 1)

    def conv_chunk(cc, carry):
        for r in range(8):
            span = ts + 8 * ((CONV_WIDTH - 1 - r) // 8)
            vbuf[r, 0:span, :] = ubuf[cc, first + r:first + r + span, :]
        for i in range(ts // rows):
            acc = None
            for tap in range(CONV_WIDTH):
                r0 = i * rows + 8 * (tap // 8)
                term = dww_ref[cc, tap:tap + 1, :] * vbuf[tap % 8, r0:r0 + rows, :]
                acc = term if acc is None else acc + term
            cbuf[cc, i * rows:(i + 1) * rows, :] = acc
        return carry

    lax.fori_loop(0, nc, conv_chunk, 0)
    v = jnp.concatenate([cbuf[cc] for cc in range(nc)], axis=-1) + dwb_ref[...]
    mu = jnp.mean(v, axis=-1, keepdims=True)
    vc = v - mu
    var = jnp.mean(vc * vc, axis=-1, keepdims=True)
    v = _silu(vc * lax.rsqrt(var + NORM_EPS) * lng_ref[...] + lnb_ref[...])
    y = jnp.dot(v.astype(BF16), w2_ref[...], preferred_element_type=F32) + b2_ref[...]
    _mixer_tail(x, y, mod_ref, n2g_ref[...], rw_ref, rb_ref, x_out, h_out, g_out)


def _conv_mixer(x, mod, n1g, w1, b1, dww, dwb, lng, lnb, w2, b2, n2g, rw, rb, ts=512, rows=64):
    b, s, d = x.shape
    nc = d // LANES
    const = lambda i, j: (0, 0)
    row = lambda a: a.reshape(1, -1)
    dww = dww.reshape(CONV_WIDTH, nc, LANES).transpose(1, 0, 2)
    out_specs, out_shape = _tail_out_specs(b, s, d, ts)
    return pl.pallas_call(
        functools.partial(_conv_mixer_kernel, ts=ts, rows=rows),
        grid=(b, s // ts),
        in_specs=[
            pl.BlockSpec((1, ts, d), lambda i, j: (i, j, 0)),
            pl.BlockSpec((1, 6, d), lambda i, j: (i, 0, 0)),
            pl.BlockSpec((1, d), const),
            pl.BlockSpec(w1.shape, const),
            pl.BlockSpec((1, 2 * d), const),
            pl.BlockSpec(dww.shape, lambda i, j: (0, 0, 0)),
            pl.BlockSpec((1, d), const),
            pl.BlockSpec((1, d), const),
            pl.BlockSpec((1, d), const),
            pl.BlockSpec(w2.shape, const),
            pl.BlockSpec((1, d), const),
            pl.BlockSpec((1, d), const),
            pl.BlockSpec(rw.shape, const),
            pl.BlockSpec((N_EXPERTS, 1), const),
        ],
        out_specs=out_specs,
        out_shape=out_shape,
        scratch_shapes=[pltpu.VMEM((nc, ts + CONV_HALO, LANES), F32),
                        pltpu.VMEM((8, ts + CONV_HALO, LANES), F32),
                        pltpu.VMEM((nc, ts, LANES), F32)],
        compiler_params=_params("arbitrary", "arbitrary"),
        name="conv_mixer_router",
    )(x, mod, row(n1g), w1, row(b1), dww, row(dwb), row(lng), row(lnb), w2, row(b2), row(n2g), rw, rb)


MOE_TILE = 512


def _dispatch_plan(route, tm):
    t = route.shape[1]
    n_tiles_max = TOP_K * t // tm + N_EXPERTS
    i32 = jnp.int32
    e = route[0:TOP_K].astype(i32)
    onehot = (e[:, :, None] == jnp.arange(N_EXPERTS, dtype=i32)).astype(i32)
    csum = jnp.cumsum(onehot, axis=1)
    cnt_slot = csum[:, -1, :]
    cnt = cnt_slot[0] + cnt_slot[1]
    padded = (cnt + tm - 1) // tm * tm
    pad_end = jnp.cumsum(padded)
    off = pad_end - padded
    first_row = off[None, :] + jnp.stack([jnp.zeros_like(cnt), cnt_slot[0]])
    pos = jnp.sum(onehot * (csum - onehot + first_row[:, None, :]), axis=-1)
    n_tiles = (pad_end[-1] // tm).astype(i32)
    tile = jnp.minimum(jnp.arange(n_tiles_max, dtype=i32), n_tiles - 1)
    tile_expert = jnp.searchsorted(pad_end // tm, tile, side="right").astype(i32)
    n_steps = t // tm
    pos_tiles = pos.reshape(TOP_K, n_steps, tm).transpose(1, 0, 2).reshape(n_steps, 1, TOP_K * tm)
    return pos_tiles, tile_expert, n_tiles.reshape(1)


def _gather_pipeline_step(step, n_steps, idx_hbm, src_hbm, idx_smem, buf, isem, gsem, rows):
    n_steps = jnp.asarray(n_steps, jnp.int32)

    def idx_copy(s, sl):
        return pltpu.make_async_copy(idx_hbm.at[s], idx_smem.at[sl], isem.at[sl])

    def start_rows(sl):
        def body(g, carry):
            base = pl.multiple_of(g * SUBLANES, SUBLANES)
            for j in range(SUBLANES):
                row = idx_smem[sl, 0, base + j]
                pltpu.make_async_copy(src_hbm.at[row >> 3, pl.ds(row & (SUBLANES - 1), 1)],
                                      buf.at[sl, g, pl.ds(j, 1)], gsem.at[sl]).start()
            return carry

        lax.fori_loop(0, rows // SUBLANES, body, 0)

    def pipeline(slot):
        nslot = 1 - slot

        @pl.when(step == 0)
        def _():
            first = idx_copy(0, 0)
            first.start()
            first.wait()
            start_rows(0)

            @pl.when(n_steps > 1)
            def _():
                idx_copy(1, 1).start()

        @pl.when(step + 1 < n_steps)
        def _():
            idx_copy(step + 1, nslot).wait()
            start_rows(nslot)

            @pl.when(step + 2 < n_steps)
            def _():
                idx_copy(step + 2, slot).start()

        @pl.when(step < n_steps)
        def _():
            pltpu.make_async_copy(src_hbm.at[pl.ds(0, rows // SUBLANES)], buf.at[slot], gsem.at[slot]).wait()

    for parity in range(2):
        pl.when(step % 2 == parity)(functools.partial(pipeline, parity))


def _gather_scratch(rows, words):
    return [pltpu.SMEM((2, 1, rows), jnp.int32), pltpu.VMEM((2, rows // SUBLANES, SUBLANES, words), jnp.uint32),
            pltpu.SemaphoreType.DMA((2,)), pltpu.SemaphoreType.DMA((2,))]


def _moe_dispatch_kernel(pos_hbm, h_ref, xs_zero_hbm, xs_hbm, idx_smem, hbuf, isem, ssem, *, tm, n_steps):
    del xs_zero_hbm
    i = pl.program_id(0) * pl.num_programs(1) + pl.program_id(1)
    groups = tm // SUBLANES

    def idx_copy(s, sl):
        return pltpu.make_async_copy(pos_hbm.at[s], idx_smem.at[sl], isem.at[sl])

    def wait_rows(sl):
        for _ in range(TOP_K):
            pltpu.make_async_copy(hbuf.at[sl], xs_hbm.at[pl.ds(0, groups)], ssem.at[sl]).wait()

    def step(slot):
        nslot = 1 - slot

        @pl.when(i == 0)
        def _():
            idx_copy(0, 0).start()

        idx_copy(i, slot).wait()

        @pl.when(i + 1 < n_steps)
        def _():
            idx_copy(i + 1, nslot).start()

        hbuf[slot] = h_ref[0].reshape(groups, SUBLANES, -1)

        def body(g, carry):
            base = pl.multiple_of(g * SUBLANES, SUBLANES)
            for j in range(SUBLANES):
                for k in range(TOP_K):
                    row = idx_smem[slot, 0, k * tm + base + j]
                    pltpu.make_async_copy(hbuf.at[slot, g, pl.ds(j, 1)],
                                          xs_hbm.at[row >> 3, pl.ds(row & (SUBLANES - 1), 1)],
                                          ssem.at[slot]).start()
            return carry

        lax.fori_loop(0, groups, body, 0)

        @pl.when(i > 0)
        def _():
            wait_rows(nslot)

        @pl.when(i == n_steps - 1)
        def _():
            wait_rows(slot)

    for parity in range(2):
        pl.when(i % 2 == parity)(functools.partial(step, parity))


def _moe_dispatch(pos_tiles, h_words, n_tiles_max, tm):
    b, s, words = h_words.shape
    groups = tm // SUBLANES
    xs_shape = (n_tiles_max * groups, SUBLANES, words)
    return pl.pallas_call(
        functools.partial(_moe_dispatch_kernel, tm=tm, n_steps=b * s // tm),
        grid=(b, s // tm),
        in_specs=[
            pl.BlockSpec(memory_space=pl.ANY),
            pl.BlockSpec((1, tm, words), lambda i, j: (i, j, 0)),
            pl.BlockSpec(memory_space=pl.ANY),
        ],
        out_specs=pl.BlockSpec(memory_space=pl.ANY),
        out_shape=jax.ShapeDtypeStruct(xs_shape, jnp.uint32),
        input_output_aliases={2: 0},
        scratch_shapes=[pltpu.SMEM((2, 1, TOP_K * tm), jnp.int32),
                        pltpu.VMEM((2, groups, SUBLANES, words), jnp.uint32),
                        pltpu.SemaphoreType.DMA((2,)), pltpu.SemaphoreType.DMA((2,))],
        compiler_params=_params("arbitrary", "arbitrary"),
        name="moe_dispatch",
    )(pos_tiles, h_words, jnp.zeros(xs_shape, jnp.uint32))


def _moe_expert_kernel(te_ref, nt_ref, x_ref, wgu_ref, wd_ref, y_ref):
    i = pl.program_id(0)
    ff = wd_ref.shape[1]
    half = wgu_ref.shape[1] // 2

    @pl.when(i < nt_ref[0])
    def _():
        words = x_ref[...]
        x = _unpack_bf16_pairs(words[:, :half]).astype(BF16)
        meta = lax.bitcast_convert_type(words[:, half:], F32)
        expert = te_ref[i].astype(F32)
        w = jnp.where(meta[:, 0:1] == expert, meta[:, 2:3],
                      jnp.where(meta[:, 1:2] == expert, meta[:, 3:4], 0.0))
        gu = jnp.dot(x, wgu_ref[0], preferred_element_type=F32)
        a = _silu(gu[:, :ff]) * gu[:, ff:] * w
        y = jnp.dot(a.astype(BF16), wd_ref[0], preferred_element_type=F32)
        y_ref[...] = _pack_bf16_pairs(y)

    @pl.when(i >= nt_ref[0])
    def _():
        y_ref[...] = jnp.zeros(y_ref.shape, jnp.uint32)


def _moe_experts(tile_expert, n_tiles, xs, wgu, wd, tm):
    n_tiles_max = tile_expert.shape[0]
    words = xs.shape[1]
    d, ff = wgu.shape[1], wd.shape[1]
    grid_spec = pltpu.PrefetchScalarGridSpec(
        num_scalar_prefetch=2,
        grid=(n_tiles_max,),
        in_specs=[
            pl.BlockSpec((tm, words), lambda i, te, nt: (i, 0)),
            pl.BlockSpec((1, d, 2 * ff), lambda i, te, nt: (te[i], 0, 0)),
            pl.BlockSpec((1, ff, d), lambda i, te, nt: (te[i], 0, 0)),
        ],
        out_specs=pl.BlockSpec((tm, d // 2), lambda i, te, nt: (i, 0)),
    )
    return pl.pallas_call(
        _moe_expert_kernel,
        grid_spec=grid_spec,
        out_shape=jax.ShapeDtypeStruct((n_tiles_max * tm, d // 2), jnp.uint32),
        compiler_params=_params("arbitrary"),
        name="moe_experts",
    )(tile_expert, n_tiles, xs, wgu, wd)


def _moe_combine_kernel(pos_hbm, y_hbm, x_ref, mod_ref, o_ref, idx_smem, ybuf, isem, gsem, *, tm, n_steps):
    i = pl.program_id(0) * pl.num_programs(1) + pl.program_id(1)
    _gather_pipeline_step(i, n_steps, pos_hbm, y_hbm, idx_smem, ybuf, isem, gsem, TOP_K * tm)
    rows = _unpack_bf16_pairs(ybuf[i % 2].reshape(TOP_K * tm, -1))
    o_ref[0] = x_ref[0] + mod_ref[0, 5:6, :] * (rows[:tm] + rows[tm:])


def _moe_combine(pos_tiles, y_words, x, mod, tm):
    b, s, d = x.shape
    n_steps = b * s // tm
    words = y_words.shape[1]
    return pl.pallas_call(
        functools.partial(_moe_combine_kernel, tm=tm, n_steps=n_steps),
        grid=(b, s // tm),
        in_specs=[
            pl.BlockSpec(memory_space=pl.ANY),
            pl.BlockSpec(memory_space=pl.ANY),
            pl.BlockSpec((1, tm, d), lambda i, j: (i, j, 0)),
            pl.BlockSpec((1, 6, d), lambda i, j: (i, 0, 0)),
        ],
        out_specs=pl.BlockSpec((1, tm, d), lambda i, j: (i, j, 0)),
        out_shape=jax.ShapeDtypeStruct((b, s, d), F32),
        scratch_shapes=_gather_scratch(TOP_K * tm, words),
        compiler_params=_params("arbitrary", "arbitrary"),
        name="moe_combine",
    )(pos_tiles, y_words.reshape(-1, SUBLANES, words), x, mod)


def _moe_layer(h_words, route, x1, mod, w_gate, w_up, w_down):
    pos_tiles, tile_expert, n_tiles = _dispatch_plan(route, MOE_TILE)
    wgu = jnp.concatenate([w_gate, w_up], axis=-1).astype(BF16)
    xs = _moe_dispatch(pos_tiles, h_words, tile_expert.shape[0], MOE_TILE)
    xs = xs.reshape(-1, xs.shape[-1])
    y_words = _moe_experts(tile_expert, n_tiles, xs, wgu, w_down.astype(BF16), MOE_TILE)
    return _moe_combine(pos_tiles, y_words, x1, mod, MOE_TILE)


def kernel(x, c, positions, ada_w, ada_b, norm1_g, norm2_g, mla_w_dqkv, mla_q_norm_g, mla_w_uq, mla_kv_norm_g, mla_w_ukv, mla_qk_q_g, mla_qk_k_g, mla_w_o, conv_pw1_w, conv_pw1_b, conv_dw_w, conv_dw_b, conv_ln_g, conv_ln_b, conv_pw2_w, conv_pw2_b, router_w, router_bias, moe_w_gate, moe_w_up, moe_w_down):
    b, s, d = x.shape
    depth = ada_w.shape[0]
    mod_all = _adaln(c, ada_w, ada_b).reshape(depth, b, 6, d)

    rw_t = router_w.astype(F32).T
    rw_hi = rw_t.astype(BF16)
    rw_lo = (rw_t - rw_hi.astype(F32)).astype(BF16)
    rw = jnp.concatenate([rw_hi, rw_lo], axis=0)
    rb = router_bias.astype(F32).reshape(N_EXPERTS, 1)

    for i in range(depth):
        mod = mod_all[i]
        jdx = i // 2
        if i % 2 == 0:
            w_uq = mla_w_uq[jdx].reshape(Q_LORA, N_HEADS, QK_HEAD)
            wuq = jnp.concatenate([w_uq[:, :, :QK_NOPE].reshape(Q_LORA, -1),
                                   w_uq[:, :, QK_NOPE:].reshape(Q_LORA, -1)], axis=-1).astype(BF16)
            w_ukv = mla_w_ukv[jdx].reshape(KV_LORA, N_HEADS, QK_NOPE + V_HEAD)
            wuk = w_ukv[:, :, :QK_NOPE].reshape(KV_LORA, -1).astype(BF16)
            wuvt = w_ukv[:, :, QK_NOPE:].reshape(KV_LORA, -1).T.astype(BF16)
            q, k, vt = _mla_proj(x, mod, norm1_g[i], mla_w_dqkv[jdx].astype(BF16), mla_q_norm_g[jdx], wuq,
                                 mla_kv_norm_g[jdx], wuk, wuvt, mla_qk_q_g[jdx], mla_qk_k_g[jdx], positions,
                                 ts=ATTN_TILE)
            o = _flash_attention(q, k, vt, t=ATTN_TILE)
            x1, h2, gates_rows = _attn_out(o, x, mod, mla_w_o[jdx].astype(BF16), norm2_g[i], rw, rb)
        else:
            x1, h2, gates_rows = _conv_mixer(
                x, mod, norm1_g[i], conv_pw1_w[jdx].astype(BF16), conv_pw1_b[jdx], conv_dw_w[jdx],
                conv_dw_b[jdx], conv_ln_g[jdx], conv_ln_b[jdx], conv_pw2_w[jdx].astype(BF16),
                conv_pw2_b[jdx], norm2_g[i], rw, rb)
        x = _moe_layer(h2, gates_rows, x1, mod, moe_w_gate[i], moe_w_up[i], moe_w_down[i])
    return x
```

```python
import functools

import jax
import jax.numpy as jnp
from jax import lax
from jax.experimental import pallas as pl
from jax.experimental.pallas import tpu as pltpu

F32 = jnp.float32
BF16 = jnp.bfloat16

N_HEADS = 8
QK_NOPE = 128
QK_ROPE = 64
QK_HEAD = QK_NOPE + QK_ROPE
V_HEAD = 128
Q_LORA = 768
KV_LORA = 256
ROPE_THETA = 10000.0
CONV_WIDTH = 31
CONV_HALO = 32
N_EXPERTS = 16
N_GROUPS = 4
EXPERTS_PER_GROUP = N_EXPERTS // N_GROUPS
TOP_K = 2
ROUTE_ROWS = 8
NORM_EPS = 1e-6
LOG2_E = 1.4426950408889634
LANES = 128
SUBLANES = 8
PROJ_TILE = 256
ATTN_TILE = 512
VMEM_LIMIT_BYTES = 52 * 1024 * 1024


def _params(*semantics):
    return pltpu.CompilerParams(dimension_semantics=semantics, vmem_limit_bytes=VMEM_LIMIT_BYTES)


def _rms(x):
    return x * lax.rsqrt(jnp.mean(x * x, axis=-1, keepdims=True) + NORM_EPS)


def _silu(x):
    return x * jax.nn.sigmoid(x)


def _modulated_norm(x, g, scale, shift):
    return (_rms(x) * g) * (1.0 + scale) + shift


def _adaln_kernel(c_ref, w_ref, b_ref, o_ref):
    a = _silu(c_ref[...]).astype(BF16)
    o_ref[0] = jnp.dot(a, w_ref[0].astype(BF16), preferred_element_type=F32) + b_ref[0]


def _adaln(c, ada_w, ada_b, tn=1536):
    depth, d, n = ada_w.shape
    b = c.shape[0]
    return pl.pallas_call(
        _adaln_kernel,
        grid=(depth, n // tn),
        in_specs=[
            pl.BlockSpec((b, d), lambda l, j: (0, 0)),
            pl.BlockSpec((1, d, tn), lambda l, j: (l, 0, j)),
            pl.BlockSpec((1, 1, tn), lambda l, j: (l, 0, j)),
        ],
        out_specs=pl.BlockSpec((1, b, tn), lambda l, j: (l, 0, j)),
        out_shape=jax.ShapeDtypeStruct((depth, b, n), F32),
        compiler_params=_params("arbitrary", "arbitrary"),
        name="adaln",
    )(c, ada_w, ada_b.reshape(depth, 1, n))


def _rope128(x, cos, sin_signed, lo_mask):
    swapped = jnp.where(lo_mask, pltpu.roll(x, LANES - QK_ROPE // 2, 1), pltpu.roll(x, QK_ROPE // 2, 1))
    return x * cos + swapped * sin_signed


def _mla_proj_kernel(x_ref, mod_ref, n1g_ref, wd_ref, qg_ref, wuq_ref, kvg_ref, wuk_ref, wuvt_ref,
                     gq_ref, gk_ref, pos_ref, invf_ref, q_out, k_out, vt_out):
    x = x_ref[0]
    h = _modulated_norm(x, n1g_ref[...], mod_ref[0, 1:2, :], mod_ref[0, 0:1, :]).astype(BF16)
    lat = jnp.dot(h, wd_ref[...], preferred_element_type=F32)
    c_q = (_rms(lat[:, :Q_LORA]) * qg_ref[...]).astype(BF16)
    c_kv = (_rms(lat[:, Q_LORA:Q_LORA + KV_LORA]) * kvg_ref[...]).astype(BF16)
    k_pe = lat[:, Q_LORA + KV_LORA:]
    q = jnp.dot(c_q, wuq_ref[...], preferred_element_type=F32)
    k_nope = jnp.dot(c_kv, wuk_ref[...], preferred_element_type=F32)
    v_t = lax.dot_general(wuvt_ref[...], c_kv, (((1,), (1,)), ((), ())), preferred_element_type=F32)

    ang = pos_ref[0].astype(F32) * invf_ref[...]
    cos = jnp.cos(ang)
    lane = lax.broadcasted_iota(jnp.int32, (1, LANES), 1)
    lo_mask = (lane % QK_ROPE) < (QK_ROPE // 2)
    sin_signed = jnp.where(lo_mask, -1.0, 1.0) * jnp.sin(ang)

    gq = gq_ref[...]
    gk = gk_ref[...]
    gq_rope2 = jnp.concatenate([gq[:, QK_NOPE:], gq[:, QK_NOPE:]], axis=-1)
    gk_rope2 = jnp.concatenate([gk[:, QK_NOPE:], gk[:, QK_NOPE:]], axis=-1)
    sm_scale = QK_HEAD ** -0.5 * LOG2_E
    nope_all = N_HEADS * QK_NOPE

    k_pe2 = jnp.concatenate([k_pe, k_pe], axis=-1)
    k_rope = _rope128(k_pe2 * gk_rope2, cos, sin_signed, lo_mask)[:, :QK_ROPE]
    ss_pe = jnp.sum(k_pe * k_pe, axis=-1, keepdims=True)

    for pair in range(N_HEADS // 2):
        q_rope_pair = q[:, nope_all + pair * LANES: nope_all + (pair + 1) * LANES]
        q_roped_pair = _rope128(q_rope_pair * gq_rope2, cos, sin_signed, lo_mask)
        for sub in range(2):
            hd = 2 * pair + sub
            qn = q[:, hd * QK_NOPE:(hd + 1) * QK_NOPE]
            qr = q_rope_pair[:, sub * QK_ROPE:(sub + 1) * QK_ROPE]
            ss = jnp.sum(qn * qn, axis=-1, keepdims=True) + jnp.sum(qr * qr, axis=-1, keepdims=True)
            r = lax.rsqrt(ss * (1.0 / QK_HEAD) + NORM_EPS) * sm_scale
            q_out[0, hd, :, 0:QK_NOPE] = (qn * r * gq[:, :QK_NOPE]).astype(BF16)
            q_out[0, hd, :, QK_NOPE:QK_HEAD] = (
                q_roped_pair[:, sub * QK_ROPE:(sub + 1) * QK_ROPE] * r).astype(BF16)

            kn = k_nope[:, hd * QK_NOPE:(hd + 1) * QK_NOPE]
            rk = lax.rsqrt((jnp.sum(kn * kn, axis=-1, keepdims=True) + ss_pe) * (1.0 / QK_HEAD) + NORM_EPS)
            k_out[0, hd, :, 0:QK_NOPE] = (kn * rk * gk[:, :QK_NOPE]).astype(BF16)
            k_out[0, hd, :, QK_NOPE:QK_HEAD] = (k_rope * rk).astype(BF16)
            vt_out[0, hd, 0] = v_t[hd * V_HEAD:(hd + 1) * V_HEAD, :].astype(BF16)


def _mla_proj(x, mod, n1g, wd, qg, wuq, kvg, wuk, wuvt, gq, gk, positions, ts, vt_tile):
    b, s, d = x.shape
    per = vt_tile // ts
    half = QK_ROPE // 2
    inv_freq = 1.0 / (ROPE_THETA ** (jnp.arange(half, dtype=F32) * (2.0 / QK_ROPE)))
    invf = jnp.tile(inv_freq, LANES // half).reshape(1, LANES)
    const = lambda i, j: (0, 0)
    return pl.pallas_call(
        _mla_proj_kernel,
        grid=(b, s // ts),
        in_specs=[
            pl.BlockSpec((1, ts, d), lambda i, j: (i, j, 0)),
            pl.BlockSpec((1, 6, d), lambda i, j: (i, 0, 0)),
            pl.BlockSpec((1, d), const),
            pl.BlockSpec(wd.shape, const),
            pl.BlockSpec((1, Q_LORA), const),
            pl.BlockSpec(wuq.shape, const),
            pl.BlockSpec((1, KV_LORA), const),
            pl.BlockSpec(wuk.shape, const),
            pl.BlockSpec(wuvt.shape, const),
            pl.BlockSpec((1, QK_HEAD), const),
            pl.BlockSpec((1, QK_HEAD), const),
            pl.BlockSpec((1, ts, 1), lambda i, j: (i, j, 0)),
            pl.BlockSpec((1, LANES), const),
        ],
        out_specs=[
            pl.BlockSpec((1, N_HEADS, ts, QK_HEAD), lambda i, j: (i, 0, j, 0)),
            pl.BlockSpec((1, N_HEADS, ts, QK_HEAD), lambda i, j: (i, 0, j, 0)),
            pl.BlockSpec((1, N_HEADS, 1, V_HEAD, ts), lambda i, j: (i, 0, j // per, 0, j % per)),
        ],
        out_shape=[
            jax.ShapeDtypeStruct((b, N_HEADS, s, QK_HEAD), BF16),
            jax.ShapeDtypeStruct((b, N_HEADS, s, QK_HEAD), BF16),
            jax.ShapeDtypeStruct((b, N_HEADS, s // vt_tile, V_HEAD, vt_tile), BF16),
        ],
        compiler_params=_params("arbitrary", "arbitrary"),
        name="mla_proj",
    )(x, mod, n1g.reshape(1, d), wd, qg.reshape(1, -1), wuq, kvg.reshape(1, -1), wuk, wuvt,
      gq.reshape(1, -1), gk.reshape(1, -1), positions.reshape(b, s, 1), invf)


def _flash_kernel(q_ref, k_ref, vt_ref, o_ref, m_ref, l_ref, acc_ref, *stage_refs, t):
    qi = pl.program_id(2)
    heads = q_ref.shape[1]
    s_refs, mb_refs = stage_refs[:heads], stage_refs[heads:]
    nt = (((1,), (1,)), ((), ()))

    def scores(kb, hh, masked, slot=None):
        slot = kb % 2 if slot is None else slot
        start = pl.multiple_of(kb * t, t)
        s = lax.dot_general(k_ref[0, hh, pl.ds(start, t), :], q_ref[0, hh], nt,
                            preferred_element_type=F32)
        if masked:
            kv_pos = lax.broadcasted_iota(jnp.int32, (t, t), 0)
            q_pos = lax.broadcasted_iota(jnp.int32, (t, t), 1)
            s = jnp.where(kv_pos <= q_pos, s, -jnp.inf)
        s_refs[hh][slot] = s
        mb_refs[hh][slot] = jnp.max(s, axis=0, keepdims=True)

    def softmax_pv(kb, hh, slot=None):
        slot = kb % 2 if slot is None else slot
        m = m_ref[hh]
        m_new = jnp.maximum(m, mb_refs[hh][slot])
        alpha = jnp.exp2(m - m_new)
        p = jnp.exp2(s_refs[hh][slot] - m_new)
        l_ref[hh] = alpha * l_ref[hh] + jnp.sum(p, axis=0, keepdims=True)
        m_ref[hh] = m_new
        pv = jnp.dot(vt_ref[0, hh, kb], p.astype(BF16), preferred_element_type=F32)
        acc_ref[hh] = alpha * acc_ref[hh] + pv

    def stage(softmax_kb, scores_kb, masked, softmax_slot=None):
        scores_slot = None if softmax_slot is None else 1 - softmax_slot
        for hh in range(heads):
            if softmax_kb is not None:
                softmax_pv(softmax_kb, hh, softmax_slot)
            if scores_kb is not None:
                scores(scores_kb, hh, masked, scores_slot)

    m_ref[...] = jnp.full(m_ref.shape, -jnp.inf, F32)
    l_ref[...] = jnp.zeros(l_ref.shape, F32)
    acc_ref[...] = jnp.zeros(acc_ref.shape, F32)

    @pl.when(qi == 0)
    def _():
        stage(None, qi, True)

    @pl.when(qi > 0)
    def _():
        stage(None, 0, False)

        steady = qi - 1

        def body(pair, carry):
            kb = 2 * pair
            stage(kb, kb + 1, False, softmax_slot=0)
            stage(kb + 1, kb + 2, False, softmax_slot=1)
            return carry

        lax.fori_loop(0, steady // 2, body, 0)

        @pl.when(steady % 2 == 1)
        def _():
            stage(steady - 1, steady, False)

        stage(qi - 1, qi, True)

    stage(qi, None, False)
    for hh in range(heads):
        o_ref[0, :, hh * V_HEAD:(hh + 1) * V_HEAD] = (acc_ref[hh] * (1.0 / l_ref[hh])).T.astype(BF16)


def _flash_attention(q, k, vt, t, heads=2):
    b, h, s, dqk = q.shape
    dv = vt.shape[-2]
    assert vt.shape[-1] == t and h % heads == 0
    return pl.pallas_call(
        functools.partial(_flash_kernel, t=t),
        grid=(b, h // heads, s // t),
        in_specs=[
            pl.BlockSpec((1, heads, t, dqk), lambda i, j, n: (i, j, n, 0)),
            pl.BlockSpec((1, heads, s, dqk), lambda i, j, n: (i, j, 0, 0)),
            pl.BlockSpec((1, heads, s // t, dv, t), lambda i, j, n: (i, j, 0, 0, 0)),
        ],
        out_specs=pl.BlockSpec((1, t, heads * dv), lambda i, j, n: (i, n, j)),
        out_shape=jax.ShapeDtypeStruct((b, s, h * dv), BF16),
        compiler_params=_params("arbitrary", "arbitrary", "arbitrary"),
        scratch_shapes=(
            [pltpu.VMEM((heads, 1, t), F32),
             pltpu.VMEM((heads, 1, t), F32),
             pltpu.VMEM((heads, dv, t), F32)]
            + [pltpu.VMEM((2, t, t), F32)] * heads
            + [pltpu.VMEM((2, 1, t), F32)] * heads),
        name="flash_attention",
    )(q, k, vt)


def _route_rows(logits, bias_col):
    scores = jax.nn.sigmoid(logits)
    biased = scores + bias_col
    b_rows = [biased[i:i + 1, :] for i in range(N_EXPERTS)]
    s_rows = [scores[i:i + 1, :] for i in range(N_EXPERTS)]
    in_top = []
    group_score = []
    for g in range(N_GROUPS):
        grp = b_rows[g * EXPERTS_PER_GROUP:(g + 1) * EXPERTS_PER_GROUP]
        gs = None
        for i in range(EXPERTS_PER_GROUP):
            rank = None
            for j in range(EXPERTS_PER_GROUP):
                if j == i:
                    continue
                beats = (grp[j] >= grp[i]) if j < i else (grp[j] > grp[i])
                cnt = jnp.where(beats, 1.0, 0.0)
                rank = cnt if rank is None else rank + cnt
            sel = rank < float(TOP_K)
            in_top.append(sel)
            term = jnp.where(sel, grp[i], 0.0)
            gs = term if gs is None else gs + term
        group_score.append(gs)
    picked = []
    for g in range(N_GROUPS):
        lost = None
        for g2 in range(N_GROUPS):
            if g2 == g:
                continue
            beats = (group_score[g2] >= group_score[g]) if g2 < g else (group_score[g2] > group_score[g])
            cnt = jnp.where(beats, 1.0, 0.0)
            lost = cnt if lost is None else lost + cnt
        picked.append(lost < 1.0)
    e_lo = e_hi = w_lo = w_hi = denom = None
    for i in range(N_EXPERTS):
        sel = jnp.where(picked[i // EXPERTS_PER_GROUP], jnp.where(in_top[i], 1.0, 0.0), 0.0)
        w = sel * s_rows[i]
        if i == 0:
            seen = sel
            e_lo, e_hi, w_lo, w_hi, denom = jnp.zeros_like(w), jnp.zeros_like(w), w, jnp.zeros_like(w), w
        else:
            first = sel * (1.0 - seen)
            second = sel * seen
            e_lo = e_lo + first * float(i)
            e_hi = e_hi + second * float(i)
            w_lo = w_lo + first * s_rows[i]
            w_hi = w_hi + second * s_rows[i]
            denom = denom + w
            seen = jnp.maximum(seen, sel)
    inv = 1.0 / denom
    pad = jnp.zeros((ROUTE_ROWS - 4,) + e_lo.shape[1:], F32)
    return jnp.concatenate([e_lo, e_hi, w_lo * inv, w_hi * inv, pad], axis=0)


def _split_bf16(x):
    hi = x.astype(BF16)
    lo = (x - hi.astype(F32)).astype(BF16)
    return hi, lo


def _pack_bf16_pairs(x):
    half = x.shape[-1] // 2
    bits = lax.bitcast_convert_type(x.astype(BF16).astype(F32), jnp.uint32)
    return (bits[:, :half] >> 16) | (bits[:, half:] & jnp.uint32(0xFFFF0000))


def _unpack_bf16_pairs(words):
    lo = lax.bitcast_convert_type(words << 16, F32)
    hi = lax.bitcast_convert_type(words & jnp.uint32(0xFFFF0000), F32)
    return jnp.concatenate([lo, hi], axis=-1)


def _mixer_tail(x, y, mod_ref, n2g, rw_ref, rb_ref, x_out, h_out, g_out):
    x1 = x + mod_ref[0, 2:3, :] * y
    x_out[0] = x1
    h2 = _modulated_norm(x1, n2g, mod_ref[0, 4:5, :], mod_ref[0, 3:4, :])
    h_hi, h_lo = _split_bf16(h2)
    half = h2.shape[-1] // 2
    h_out[0, :, 0:half] = _pack_bf16_pairs(h2)
    nt = (((1,), (1,)), ((), ()))
    a = lax.dot_general(rw_ref[...], h_hi, nt, preferred_element_type=F32)
    c = lax.dot_general(rw_ref[0:N_EXPERTS, :], h_lo, nt, preferred_element_type=F32)
    logits = a[0:N_EXPERTS, :] + a[N_EXPERTS:, :] + c
    route = _route_rows(logits, rb_ref[...])
    g_out[...] = route
    pad = jnp.zeros((LANES - ROUTE_ROWS, route.shape[1]), F32)
    meta = jnp.concatenate([route, pad], axis=0).T
    h_out[0, :, half:half + LANES] = lax.bitcast_convert_type(meta, jnp.uint32)


def _attn_out_kernel(o_ref, x_ref, mod_ref, wo_ref, n2g_ref, rw_ref, rb_ref, x_out, h_out, g_out):
    y = jnp.dot(o_ref[0], wo_ref[...], preferred_element_type=F32)
    _mixer_tail(x_ref[0], y, mod_ref, n2g_ref[...], rw_ref, rb_ref, x_out, h_out, g_out)


def _tail_out_specs(b, s, d, tm):
    nt = s // tm
    specs = [
        pl.BlockSpec((1, tm, d), lambda i, j: (i, j, 0)),
        pl.BlockSpec((1, tm, d // 2 + LANES), lambda i, j: (i, j, 0)),
        pl.BlockSpec((ROUTE_ROWS, tm), lambda i, j: (0, i * nt + j)),
    ]
    shapes = [
        jax.ShapeDtypeStruct((b, s, d), F32),
        jax.ShapeDtypeStruct((b, s, d // 2 + LANES), jnp.uint32),
        jax.ShapeDtypeStruct((ROUTE_ROWS, b * s), F32),
    ]
    return specs, shapes


def _attn_out(o, x, mod, wo, n2g, rw, rb, tm=512):
    b, s, d = x.shape
    const = lambda i, j: (0, 0)
    out_specs, out_shape = _tail_out_specs(b, s, d, tm)
    return pl.pallas_call(
        _attn_out_kernel,
        grid=(b, s // tm),
        in_specs=[
            pl.BlockSpec((1, tm, o.shape[-1]), lambda i, j: (i, j, 0)),
            pl.BlockSpec((1, tm, d), lambda i, j: (i, j, 0)),
            pl.BlockSpec((1, 6, d), lambda i, j: (i, 0, 0)),
            pl.BlockSpec(wo.shape, const),
            pl.BlockSpec((1, d), const),
            pl.BlockSpec(rw.shape, const),
            pl.BlockSpec((N_EXPERTS, 1), const),
        ],
        out_specs=out_specs,
        out_shape=out_shape,
        compiler_params=_params("arbitrary", "arbitrary"),
        name="attn_out_router",
    )(o, x, mod, wo, n2g.reshape(1, d), rw, rb)


def _conv_mixer_kernel(x_ref, mod_ref, n1g_ref, w1_ref, b1_ref, dww_ref, dwb_ref, lng_ref, lnb_ref,
                       w2_ref, b2_ref, n2g_ref, rw_ref, rb_ref, x_out, h_out, g_out, ubuf, vbuf, cbuf,
                       *, ts, rows):
    j = pl.program_id(1)
    x = x_ref[0]
    d = x.shape[-1]
    h = _modulated_norm(x, n1g_ref[...], mod_ref[0, 1:2, :], mod_ref[0, 0:1, :]).astype(BF16)
    u = jnp.dot(h, w1_ref[...], preferred_element_type=F32) + b1_ref[...]
    u = u[:, :d] * jax.nn.sigmoid(u[:, d:])

    nc = d // LANES

    @pl.when(j == 0)
    def _():
        ubuf[:, 0:CONV_HALO, :] = jnp.zeros((nc, CONV_HALO, LANES), F32)

    @pl.when(j > 0)
    def _():
        ubuf[:, 0:CONV_HALO, :] = ubuf[:, ts:ts + CONV_HALO, :]

    for cc in range(nc):
        ubuf[cc, CONV_HALO:CONV_HALO + ts, :] = u[:, cc * LANES:(cc + 1) * LANES]

    first = CONV_HALO - (CONV_WIDTH - 1)

    def conv_chunk(cc, carry):
        for r in range(8):
            span = ts + 8 * ((CONV_WIDTH - 1 - r) // 8)
            vbuf[r, 0:span, :] = ubuf[cc, first + r:first + r + span, :]
        for i in range(ts // rows):
            acc = None
            for tap in range(CONV_WIDTH):
                r0 = i * rows + 8 * (tap // 8)
                term = dww_ref[cc, tap:tap + 1, :] * vbuf[tap % 8, r0:r0 + rows, :]
                acc = term if acc is None else acc + term
            cbuf[cc, i * rows:(i + 1) * rows, :] = acc
        return carry

    lax.fori_loop(0, nc, conv_chunk, 0)
    v = jnp.concatenate([cbuf[cc] for cc in range(nc)], axis=-1) + dwb_ref[...]
    mu = jnp.mean(v, axis=-1, keepdims=True)
    vc = v - mu
    var = jnp.mean(vc * vc, axis=-1, keepdims=True)
    v = _silu(vc * lax.rsqrt(var + NORM_EPS) * lng_ref[...] + lnb_ref[...])
    y = jnp.dot(v.astype(BF16), w2_ref[...], preferred_element_type=F32) + b2_ref[...]
    _mixer_tail(x, y, mod_ref, n2g_ref[...], rw_ref, rb_ref, x_out, h_out, g_out)


def _conv_mixer(x, mod, n1g, w1, b1, dww, dwb, lng, lnb, w2, b2, n2g, rw, rb, ts=512, rows=64):
    b, s, d = x.shape
    nc = d // LANES
    const = lambda i, j: (0, 0)
    row = lambda a: a.reshape(1, -1)
    dww = dww.reshape(CONV_WIDTH, nc, LANES).transpose(1, 0, 2)
    out_specs, out_shape = _tail_out_specs(b, s, d, ts)
    return pl.pallas_call(
        functools.partial(_conv_mixer_kernel, ts=ts, rows=rows),
        grid=(b, s // ts),
        in_specs=[
            pl.BlockSpec((1, ts, d), lambda i, j: (i, j, 0)),
            pl.BlockSpec((1, 6, d), lambda i, j: (i, 0, 0)),
            pl.BlockSpec((1, d), const),
            pl.BlockSpec(w1.shape, const),
            pl.BlockSpec((1, 2 * d), const),
            pl.BlockSpec(dww.shape, lambda i, j: (0, 0, 0)),
            pl.BlockSpec((1, d), const),
            pl.BlockSpec((1, d), const),
            pl.BlockSpec((1, d), const),
            pl.BlockSpec(w2.shape, const),
            pl.BlockSpec((1, d), const),
            pl.BlockSpec((1, d), const),
            pl.BlockSpec(rw.shape, const),
            pl.BlockSpec((N_EXPERTS, 1), const),
        ],
        out_specs=out_specs,
        out_shape=out_shape,
        scratch_shapes=[pltpu.VMEM((nc, ts + CONV_HALO, LANES), F32),
                        pltpu.VMEM((8, ts + CONV_HALO, LANES), F32),
                        pltpu.VMEM((nc, ts, LANES), F32)],
        compiler_params=_params("arbitrary", "arbitrary"),
        name="conv_mixer_router",
    )(x, mod, row(n1g), w1, row(b1), dww, row(dwb), row(lng), row(lnb), w2, row(b2), row(n2g), rw, rb)


MOE_TILE = 512
PAIRS_PER_GROUP = EXPERTS_PER_GROUP * (EXPERTS_PER_GROUP - 1) // 2
PAIR_CLASSES = N_GROUPS * PAIRS_PER_GROUP


def _class_experts():
    lo, hi = [], []
    for g in range(N_GROUPS):
        for a in range(EXPERTS_PER_GROUP):
            for b in range(a + 1, EXPERTS_PER_GROUP):
                lo.append(g * EXPERTS_PER_GROUP + a)
                hi.append(g * EXPERTS_PER_GROUP + b)
    return jnp.asarray(lo, jnp.int32), jnp.asarray(hi, jnp.int32)


def _dispatch_plan(route, tm):
    t = route.shape[1]
    n_tiles_max = t // tm + PAIR_CLASSES
    i32 = jnp.int32
    e_lo, e_hi = route[0].astype(i32), route[1].astype(i32)
    a, b = e_lo % EXPERTS_PER_GROUP, e_hi % EXPERTS_PER_GROUP
    pair = a * (2 * EXPERTS_PER_GROUP - 1 - a) // 2 + (b - a - 1)
    cls = (e_lo // EXPERTS_PER_GROUP) * PAIRS_PER_GROUP + pair
    onehot = (cls[:, None] == jnp.arange(PAIR_CLASSES, dtype=i32)).astype(i32)
    csum = jnp.cumsum(onehot, axis=0)
    cnt = csum[-1]
    padded = (cnt + tm - 1) // tm * tm
    pad_end = jnp.cumsum(padded)
    off = pad_end - padded
    pos = jnp.sum(onehot * (csum - onehot + off[None, :]), axis=-1)
    n_tiles = (pad_end[-1] // tm).astype(i32)
    tile = jnp.minimum(jnp.arange(n_tiles_max, dtype=i32), n_tiles - 1)
    tile_class = jnp.searchsorted(pad_end // tm, tile, side="right").astype(i32)
    class_lo, class_hi = _class_experts()
    return pos.reshape(t // tm, 1, tm), class_lo[tile_class], class_hi[tile_class], n_tiles.reshape(1)


def _gather_pipeline_step(step, n_steps, idx_hbm, src_hbm, idx_smem, buf, isem, gsem, rows):
    n_steps = jnp.asarray(n_steps, jnp.int32)

    def idx_copy(s, sl):
        return pltpu.make_async_copy(idx_hbm.at[s], idx_smem.at[sl], isem.at[sl])

    def start_rows(sl):
        def body(g, carry):
            base = pl.multiple_of(g * SUBLANES, SUBLANES)
            for j in range(SUBLANES):
                row = idx_smem[sl, 0, base + j]
                pltpu.make_async_copy(src_hbm.at[row >> 3, pl.ds(row & (SUBLANES - 1), 1)],
                                      buf.at[sl, g, pl.ds(j, 1)], gsem.at[sl]).start()
            return carry

        lax.fori_loop(0, rows // SUBLANES, body, 0)

    def pipeline(slot):
        nslot = 1 - slot

        @pl.when(step == 0)
        def _():
            first = idx_copy(0, 0)
            first.start()
            first.wait()
            start_rows(0)

            @pl.when(n_steps > 1)
            def _():
                idx_copy(1, 1).start()

        @pl.when(step + 1 < n_steps)
        def _():
            idx_copy(step + 1, nslot).wait()
            start_rows(nslot)

            @pl.when(step + 2 < n_steps)
            def _():
                idx_copy(step + 2, slot).start()

        @pl.when(step < n_steps)
        def _():
            pltpu.make_async_copy(src_hbm.at[pl.ds(0, rows // SUBLANES)], buf.at[slot], gsem.at[slot]).wait()

    for parity in range(2):
        pl.when(step % 2 == parity)(functools.partial(pipeline, parity))


def _gather_scratch(rows, words):
    return [pltpu.SMEM((2, 1, rows), jnp.int32), pltpu.VMEM((2, rows // SUBLANES, SUBLANES, words), jnp.uint32),
            pltpu.SemaphoreType.DMA((2,)), pltpu.SemaphoreType.DMA((2,))]


def _moe_dispatch_kernel(pos_hbm, h_ref, xs_zero_hbm, xs_hbm, idx_smem, hbuf, isem, ssem, *, tm, n_steps):
    del xs_zero_hbm
    i = pl.program_id(0) * pl.num_programs(1) + pl.program_id(1)
    groups = tm // SUBLANES

    def idx_copy(s, sl):
        return pltpu.make_async_copy(pos_hbm.at[s], idx_smem.at[sl], isem.at[sl])

    def wait_rows(sl):
        pltpu.make_async_copy(hbuf.at[sl], xs_hbm.at[pl.ds(0, groups)], ssem.at[sl]).wait()

    def step(slot):
        nslot = 1 - slot

        @pl.when(i == 0)
        def _():
            idx_copy(0, 0).start()

        idx_copy(i, slot).wait()

        @pl.when(i + 1 < n_steps)
        def _():
            idx_copy(i + 1, nslot).start()

        hbuf[slot] = h_ref[0].reshape(groups, SUBLANES, -1)

        def body(g, carry):
            base = pl.multiple_of(g * SUBLANES, SUBLANES)
            for j in range(SUBLANES):
                row = idx_smem[slot, 0, base + j]
                pltpu.make_async_copy(hbuf.at[slot, g, pl.ds(j, 1)],
                                      xs_hbm.at[row >> 3, pl.ds(row & (SUBLANES - 1), 1)],
                                      ssem.at[slot]).start()
            return carry

        lax.fori_loop(0, groups, body, 0)

        @pl.when(i > 0)
        def _():
            wait_rows(nslot)

        @pl.when(i == n_steps - 1)
        def _():
            wait_rows(slot)

    for parity in range(2):
        pl.when(i % 2 == parity)(functools.partial(step, parity))


def _moe_dispatch(pos_tiles, h_words, n_tiles_max, tm):
    b, s, words = h_words.shape
    groups = tm // SUBLANES
    xs_shape = (n_tiles_max * groups, SUBLANES, words)
    return pl.pallas_call(
        functools.partial(_moe_dispatch_kernel, tm=tm, n_steps=b * s // tm),
        grid=(b, s // tm),
        in_specs=[
            pl.BlockSpec(memory_space=pl.ANY),
            pl.BlockSpec((1, tm, words), lambda i, j: (i, j, 0)),
            pl.BlockSpec(memory_space=pl.ANY),
        ],
        out_specs=pl.BlockSpec(memory_space=pl.ANY),
        out_shape=jax.ShapeDtypeStruct(xs_shape, jnp.uint32),
        input_output_aliases={2: 0},
        scratch_shapes=[pltpu.SMEM((2, 1, tm), jnp.int32),
                        pltpu.VMEM((2, groups, SUBLANES, words), jnp.uint32),
                        pltpu.SemaphoreType.DMA((2,)), pltpu.SemaphoreType.DMA((2,))],
        compiler_params=_params("arbitrary", "arbitrary"),
        name="moe_dispatch",
    )(pos_tiles, h_words, jnp.zeros(xs_shape, jnp.uint32))


def _moe_expert_kernel(lo_ref, hi_ref, nt_ref, x_ref, wgu_lo_ref, wgu_hi_ref, wd_lo_ref, wd_hi_ref, y_ref):
    del lo_ref, hi_ref
    i = pl.program_id(0)
    ff = wd_lo_ref.shape[1]
    half = wgu_lo_ref.shape[1] // 2

    @pl.when(i < nt_ref[0])
    def _():
        words = x_ref[...]
        x = _unpack_bf16_pairs(words[:, :half]).astype(BF16)
        meta = lax.bitcast_convert_type(words[:, half:], F32)
        y = None
        for slot, (wgu_ref, wd_ref) in enumerate(((wgu_lo_ref, wd_lo_ref), (wgu_hi_ref, wd_hi_ref))):
            gu = jnp.dot(x, wgu_ref[0], preferred_element_type=F32)
            a = _silu(gu[:, :ff]) * gu[:, ff:] * meta[:, 2 + slot:3 + slot]
            part = jnp.dot(a.astype(BF16), wd_ref[0], preferred_element_type=F32)
            y = part if y is None else y + part
        y_ref[...] = _pack_bf16_pairs(y)

    @pl.when(i >= nt_ref[0])
    def _():
        y_ref[...] = jnp.zeros(y_ref.shape, jnp.uint32)


def _moe_experts(tile_lo, tile_hi, n_tiles, xs, wgu, wd, tm):
    n_tiles_max = tile_lo.shape[0]
    words = xs.shape[1]
    d, ff = wgu.shape[1], wd.shape[1]
    grid_spec = pltpu.PrefetchScalarGridSpec(
        num_scalar_prefetch=3,
        grid=(n_tiles_max,),
        in_specs=[
            pl.BlockSpec((tm, words), lambda i, lo, hi, nt: (i, 0)),
            pl.BlockSpec((1, d, 2 * ff), lambda i, lo, hi, nt: (lo[i], 0, 0)),
            pl.BlockSpec((1, d, 2 * ff), lambda i, lo, hi, nt: (hi[i], 0, 0)),
            pl.BlockSpec((1, ff, d), lambda i, lo, hi, nt: (lo[i], 0, 0)),
            pl.BlockSpec((1, ff, d), lambda i, lo, hi, nt: (hi[i], 0, 0)),
        ],
        out_specs=pl.BlockSpec((tm, d // 2), lambda i, lo, hi, nt: (i, 0)),
    )
    return pl.pallas_call(
        _moe_expert_kernel,
        grid_spec=grid_spec,
        out_shape=jax.ShapeDtypeStruct((n_tiles_max * tm, d // 2), jnp.uint32),
        compiler_params=_params("arbitrary"),
        name="moe_experts",
    )(tile_lo, tile_hi, n_tiles, xs, wgu, wgu, wd, wd)


def _moe_combine_kernel(pos_hbm, y_hbm, x_ref, mod_ref, o_ref, idx_smem, ybuf, isem, gsem, *, tm, n_steps):
    i = pl.program_id(0) * pl.num_programs(1) + pl.program_id(1)
    _gather_pipeline_step(i, n_steps, pos_hbm, y_hbm, idx_smem, ybuf, isem, gsem, tm)
    o_ref[0] = x_ref[0] + mod_ref[0, 5:6, :] * _unpack_bf16_pairs(ybuf[i % 2].reshape(tm, -1))


def _moe_combine(pos_tiles, y_words, x, mod, tm):
    b, s, d = x.shape
    n_steps = b * s // tm
    words = y_words.shape[1]
    return pl.pallas_call(
        functools.partial(_moe_combine_kernel, tm=tm, n_steps=n_steps),
        grid=(b, s // tm),
        in_specs=[
            pl.BlockSpec(memory_space=pl.ANY),
            pl.BlockSpec(memory_space=pl.ANY),
            pl.BlockSpec((1, tm, d), lambda i, j: (i, j, 0)),
            pl.BlockSpec((1, 6, d), lambda i, j: (i, 0, 0)),
        ],
        out_specs=pl.BlockSpec((1, tm, d), lambda i, j: (i, j, 0)),
        out_shape=jax.ShapeDtypeStruct((b, s, d), F32),
        scratch_shapes=_gather_scratch(tm, words),
        compiler_params=_params("arbitrary", "arbitrary"),
        name="moe_combine",
    )(pos_tiles, y_words.reshape(-1, SUBLANES, words), x, mod)


def _moe_layer(h_words, route, x1, mod, w_gate, w_up, w_down):
    pos_tiles, tile_lo, tile_hi, n_tiles = _dispatch_plan(route, MOE_TILE)
    wgu = jnp.concatenate([w_gate, w_up], axis=-1).astype(BF16)
    xs = _moe_dispatch(pos_tiles, h_words, tile_lo.shape[0], MOE_TILE)
    xs = xs.reshape(-1, xs.shape[-1])
    y_words = _moe_experts(tile_lo, tile_hi, n_tiles, xs, wgu, w_down.astype(BF16), MOE_TILE)
    return _moe_combine(pos_tiles, y_words, x1, mod, MOE_TILE)


def kernel(x, c, positions, ada_w, ada_b, norm1_g, norm2_g, mla_w_dqkv, mla_q_norm_g, mla_w_uq, mla_kv_norm_g, mla_w_ukv, mla_qk_q_g, mla_qk_k_g, mla_w_o, conv_pw1_w, conv_pw1_b, conv_dw_w, conv_dw_b, conv_ln_g, conv_ln_b, conv_pw2_w, conv_pw2_b, router_w, router_bias, moe_w_gate, moe_w_up, moe_w_down):
    b, s, d = x.shape
    depth = ada_w.shape[0]
    mod_all = _adaln(c, ada_w, ada_b).reshape(depth, b, 6, d)

    rw_t = router_w.astype(F32).T
    rw_hi = rw_t.astype(BF16)
    rw_lo = (rw_t - rw_hi.astype(F32)).astype(BF16)
    rw = jnp.concatenate([rw_hi, rw_lo], axis=0)
    rb = router_bias.astype(F32).reshape(N_EXPERTS, 1)

    for i in range(depth):
        mod = mod_all[i]
        jdx = i // 2
        if i % 2 == 0:
            w_uq = mla_w_uq[jdx].reshape(Q_LORA, N_HEADS, QK_HEAD)
            wuq = jnp.concatenate([w_uq[:, :, :QK_NOPE].reshape(Q_LORA, -1),
                                   w_uq[:, :, QK_NOPE:].reshape(Q_LORA, -1)], axis=-1).astype(BF16)
            w_ukv = mla_w_ukv[jdx].reshape(KV_LORA, N_HEADS, QK_NOPE + V_HEAD)
            wuk = w_ukv[:, :, :QK_NOPE].reshape(KV_LORA, -1).astype(BF16)
            wuvt = w_ukv[:, :, QK_NOPE:].reshape(KV_LORA, -1).T.astype(BF16)
            q, k, vt = _mla_proj(x, mod, norm1_g[i], mla_w_dqkv[jdx].astype(BF16), mla_q_norm_g[jdx], wuq,
                                 mla_kv_norm_g[jdx], wuk, wuvt, mla_qk_q_g[jdx], mla_qk_k_g[jdx], positions,
                                 ts=PROJ_TILE, vt_tile=ATTN_TILE)
            o = _flash_attention(q, k, vt, t=ATTN_TILE)
            x1, h2, gates_rows = _attn_out(o, x, mod, mla_w_o[jdx].astype(BF16), norm2_g[i], rw, rb)
        else:
            x1, h2, gates_rows = _conv_mixer(
                x, mod, norm1_g[i], conv_pw1_w[jdx].astype(BF16), conv_pw1_b[jdx], conv_dw_w[jdx],
                conv_dw_b[jdx], conv_ln_g[jdx], conv_ln_b[jdx], conv_pw2_w[jdx].astype(BF16),
                conv_pw2_b[jdx], norm2_g[i], rw, rb)
        x = _moe_layer(h2, gates_rows, x1, mod, moe_w_gate[i], moe_w_up[i], moe_w_down[i])
    return x
```

```python
import functools

import jax
import jax.numpy as jnp
from jax import lax
from jax.experimental import pallas as pl
from jax.experimental.pallas import tpu as pltpu

F32 = jnp.float32
BF16 = jnp.bfloat16

N_HEADS = 8
QK_NOPE = 128
QK_ROPE = 64
QK_HEAD = QK_NOPE + QK_ROPE
V_HEAD = 128
Q_LORA = 768
KV_LORA = 256
ROPE_THETA = 10000.0
CONV_WIDTH = 31
CONV_HALO = 32
N_EXPERTS = 16
N_GROUPS = 4
EXPERTS_PER_GROUP = N_EXPERTS // N_GROUPS
TOP_K = 2
ROUTE_ROWS = 8
NORM_EPS = 1e-6
LOG2_E = 1.4426950408889634
LANES = 128
SUBLANES = 8
PROJ_TILE = 256
ATTN_TILE = 512
VMEM_LIMIT_BYTES = 52 * 1024 * 1024


def _params(*semantics):
    return pltpu.CompilerParams(dimension_semantics=semantics, vmem_limit_bytes=VMEM_LIMIT_BYTES)


def _rms(x):
    return x * lax.rsqrt(jnp.mean(x * x, axis=-1, keepdims=True) + NORM_EPS)


def _silu(x):
    return x * jax.nn.sigmoid(x)


def _modulated_norm(x, g, scale, shift):
    return (_rms(x) * g) * (1.0 + scale) + shift


def _adaln_kernel(c_ref, w_ref, b_ref, o_ref):
    a = _silu(c_ref[...]).astype(BF16)
    o_ref[0] = jnp.dot(a, w_ref[0].astype(BF16), preferred_element_type=F32) + b_ref[0]


def _adaln(c, ada_w, ada_b, tn=1536):
    depth, d, n = ada_w.shape
    b = c.shape[0]
    return pl.pallas_call(
        _adaln_kernel,
        grid=(depth, n // tn),
        in_specs=[
            pl.BlockSpec((b, d), lambda l, j: (0, 0)),
            pl.BlockSpec((1, d, tn), lambda l, j: (l, 0, j)),
            pl.BlockSpec((1, 1, tn), lambda l, j: (l, 0, j)),
        ],
        out_specs=pl.BlockSpec((1, b, tn), lambda l, j: (l, 0, j)),
        out_shape=jax.ShapeDtypeStruct((depth, b, n), F32),
        compiler_params=_params("arbitrary", "arbitrary"),
        name="adaln",
    )(c, ada_w, ada_b.reshape(depth, 1, n))


def _rope128(x, cos, sin_signed, lo_mask):
    swapped = jnp.where(lo_mask, pltpu.roll(x, LANES - QK_ROPE // 2, 1), pltpu.roll(x, QK_ROPE // 2, 1))
    return x * cos + swapped * sin_signed


def _mla_proj_kernel(x_ref, mod_ref, n1g_ref, wd_ref, qg_ref, wuq_ref, kvg_ref, wuk_ref, wuvt_ref,
                     gq_ref, gk_ref, pos_ref, invf_ref, q_out, k_out, vt_out):
    x = x_ref[0]
    h = _modulated_norm(x, n1g_ref[...], mod_ref[0, 1:2, :], mod_ref[0, 0:1, :]).astype(BF16)
    lat = jnp.dot(h, wd_ref[...], preferred_element_type=F32)
    c_q = (_rms(lat[:, :Q_LORA]) * qg_ref[...]).astype(BF16)
    c_kv = (_rms(lat[:, Q_LORA:Q_LORA + KV_LORA]) * kvg_ref[...]).astype(BF16)
    k_pe = lat[:, Q_LORA + KV_LORA:]
    q = jnp.dot(c_q, wuq_ref[...], preferred_element_type=F32)
    k_nope = jnp.dot(c_kv, wuk_ref[...], preferred_element_type=F32)
    v_t = lax.dot_general(wuvt_ref[...], c_kv, (((1,), (1,)), ((), ())), preferred_element_type=F32)

    ang = pos_ref[0].astype(F32) * invf_ref[...]
    cos = jnp.cos(ang)
    lane = lax.broadcasted_iota(jnp.int32, (1, LANES), 1)
    lo_mask = (lane % QK_ROPE) < (QK_ROPE // 2)
    sin_signed = jnp.where(lo_mask, -1.0, 1.0) * jnp.sin(ang)

    gq = gq_ref[...]
    gk = gk_ref[...]
    gq_rope2 = jnp.concatenate([gq[:, QK_NOPE:], gq[:, QK_NOPE:]], axis=-1)
    gk_rope2 = jnp.concatenate([gk[:, QK_NOPE:], gk[:, QK_NOPE:]], axis=-1)
    sm_scale = QK_HEAD ** -0.5 * LOG2_E
    nope_all = N_HEADS * QK_NOPE

    k_pe2 = jnp.concatenate([k_pe, k_pe], axis=-1)
    k_rope = _rope128(k_pe2 * gk_rope2, cos, sin_signed, lo_mask)[:, :QK_ROPE]
    ss_pe = jnp.sum(k_pe * k_pe, axis=-1, keepdims=True)

    for pair in range(N_HEADS // 2):
        q_rope_pair = q[:, nope_all + pair * LANES: nope_all + (pair + 1) * LANES]
        q_roped_pair = _rope128(q_rope_pair * gq_rope2, cos, sin_signed, lo_mask)
        for sub in range(2):
            hd = 2 * pair + sub
            qn = q[:, hd * QK_NOPE:(hd + 1) * QK_NOPE]
            qr = q_rope_pair[:, sub * QK_ROPE:(sub + 1) * QK_ROPE]
            ss = jnp.sum(qn * qn, axis=-1, keepdims=True) + jnp.sum(qr * qr, axis=-1, keepdims=True)
            r = lax.rsqrt(ss * (1.0 / QK_HEAD) + NORM_EPS) * sm_scale
            q_out[0, hd, :, 0:QK_NOPE] = (qn * r * gq[:, :QK_NOPE]).astype(BF16)
            q_out[0, hd, :, QK_NOPE:QK_HEAD] = (
                q_roped_pair[:, sub * QK_ROPE:(sub + 1) * QK_ROPE] * r).astype(BF16)

            kn = k_nope[:, hd * QK_NOPE:(hd + 1) * QK_NOPE]
            rk = lax.rsqrt((jnp.sum(kn * kn, axis=-1, keepdims=True) + ss_pe) * (1.0 / QK_HEAD) + NORM_EPS)
            k_out[0, hd, :, 0:QK_NOPE] = (kn * rk * gk[:, :QK_NOPE]).astype(BF16)
            k_out[0, hd, :, QK_NOPE:QK_HEAD] = (k_rope * rk).astype(BF16)
            vt_out[0, hd, 0] = v_t[hd * V_HEAD:(hd + 1) * V_HEAD, :].astype(BF16)


def _mla_proj(x, mod, n1g, wd, qg, wuq, kvg, wuk, wuvt, gq, gk, positions, ts, vt_tile):
    b, s, d = x.shape
    per = vt_tile // ts
    half = QK_ROPE // 2
    inv_freq = 1.0 / (ROPE_THETA ** (jnp.arange(half, dtype=F32) * (2.0 / QK_ROPE)))
    invf = jnp.tile(inv_freq, LANES // half).reshape(1, LANES)
    const = lambda i, j: (0, 0)
    return pl.pallas_call(
        _mla_proj_kernel,
        grid=(b, s // ts),
        in_specs=[
            pl.BlockSpec((1, ts, d), lambda i, j: (i, j, 0)),
            pl.BlockSpec((1, 6, d), lambda i, j: (i, 0, 0)),
            pl.BlockSpec((1, d), const),
            pl.BlockSpec(wd.shape, const),
            pl.BlockSpec((1, Q_LORA), const),
            pl.BlockSpec(wuq.shape, const),
            pl.BlockSpec((1, KV_LORA), const),
            pl.BlockSpec(wuk.shape, const),
            pl.BlockSpec(wuvt.shape, const),
            pl.BlockSpec((1, QK_HEAD), const),
            pl.BlockSpec((1, QK_HEAD), const),
            pl.BlockSpec((1, ts, 1), lambda i, j: (i, j, 0)),
            pl.BlockSpec((1, LANES), const),
        ],
        out_specs=[
            pl.BlockSpec((1, N_HEADS, ts, QK_HEAD), lambda i, j: (i, 0, j, 0)),
            pl.BlockSpec((1, N_HEADS, ts, QK_HEAD), lambda i, j: (i, 0, j, 0)),
            pl.BlockSpec((1, N_HEADS, 1, V_HEAD, ts), lambda i, j: (i, 0, j // per, 0, j % per)),
        ],
        out_shape=[
            jax.ShapeDtypeStruct((b, N_HEADS, s, QK_HEAD), BF16),
            jax.ShapeDtypeStruct((b, N_HEADS, s, QK_HEAD), BF16),
            jax.ShapeDtypeStruct((b, N_HEADS, s // vt_tile, V_HEAD, vt_tile), BF16),
        ],
        compiler_params=_params("arbitrary", "arbitrary"),
        name="mla_proj",
    )(x, mod, n1g.reshape(1, d), wd, qg.reshape(1, -1), wuq, kvg.reshape(1, -1), wuk, wuvt,
      gq.reshape(1, -1), gk.reshape(1, -1), positions.reshape(b, s, 1), invf)


def _flash_kernel(q_ref, k_ref, vt_ref, o_ref, m_ref, l_ref, acc_ref, *stage_refs, t):
    qi = pl.program_id(2)
    heads = q_ref.shape[1]
    s_refs, mb_refs = stage_refs[:heads], stage_refs[heads:]
    nt = (((1,), (1,)), ((), ()))

    def scores(kb, hh, masked, slot=None):
        slot = kb % 2 if slot is None else slot
        start = pl.multiple_of(kb * t, t)
        s = lax.dot_general(k_ref[0, hh, pl.ds(start, t), :], q_ref[0, hh], nt,
                            preferred_element_type=F32)
        if masked:
            kv_pos = lax.broadcasted_iota(jnp.int32, (t, t), 0)
            q_pos = lax.broadcasted_iota(jnp.int32, (t, t), 1)
            s = jnp.where(kv_pos <= q_pos, s, -jnp.inf)
        s_refs[hh][slot] = s
        mb_refs[hh][slot] = jnp.max(s, axis=0, keepdims=True)

    def softmax_pv(kb, hh, slot=None):
        slot = kb % 2 if slot is None else slot
        m = m_ref[hh]
        m_new = jnp.maximum(m, mb_refs[hh][slot])
        alpha = jnp.exp2(m - m_new)
        p = jnp.exp2(s_refs[hh][slot] - m_new)
        l_ref[hh] = alpha * l_ref[hh] + jnp.sum(p, axis=0, keepdims=True)
        m_ref[hh] = m_new
        pv = jnp.dot(vt_ref[0, hh, kb], p.astype(BF16), preferred_element_type=F32)
        acc_ref[hh] = alpha * acc_ref[hh] + pv

    def stage(softmax_kb, scores_kb, masked, softmax_slot=None):
        scores_slot = None if softmax_slot is None else 1 - softmax_slot
        for hh in range(heads):
            if softmax_kb is not None:
                softmax_pv(softmax_kb, hh, softmax_slot)
            if scores_kb is not None:
                scores(scores_kb, hh, masked, scores_slot)

    m_ref[...] = jnp.full(m_ref.shape, -jnp.inf, F32)
    l_ref[...] = jnp.zeros(l_ref.shape, F32)
    acc_ref[...] = jnp.zeros(acc_ref.shape, F32)

    @pl.when(qi == 0)
    def _():
        stage(None, qi, True)

    @pl.when(qi > 0)
    def _():
        stage(None, 0, False)

        steady = qi - 1

        def body(pair, carry):
            kb = 2 * pair
            stage(kb, kb + 1, False, softmax_slot=0)
            stage(kb + 1, kb + 2, False, softmax_slot=1)
            return carry

        lax.fori_loop(0, steady // 2, body, 0)

        @pl.when(steady % 2 == 1)
        def _():
            stage(steady - 1, steady, False)

        stage(qi - 1, qi, True)

    stage(qi, None, False)
    for hh in range(heads):
        o_ref[0, :, hh * V_HEAD:(hh + 1) * V_HEAD] = (acc_ref[hh] * (1.0 / l_ref[hh])).T.astype(BF16)


def _flash_attention(q, k, vt, t, heads=4):
    b, h, s, dqk = q.shape
    dv = vt.shape[-2]
    assert vt.shape[-1] == t and h % heads == 0
    return pl.pallas_call(
        functools.partial(_flash_kernel, t=t),
        grid=(b, h // heads, s // t),
        in_specs=[
            pl.BlockSpec((1, heads, t, dqk), lambda i, j, n: (i, j, n, 0)),
            pl.BlockSpec((1, heads, s, dqk), lambda i, j, n: (i, j, 0, 0), pipeline_mode=pl.Buffered(1)),
            pl.BlockSpec((1, heads, s // t, dv, t), lambda i, j, n: (i, j, 0, 0, 0),
                         pipeline_mode=pl.Buffered(1)),
        ],
        out_specs=pl.BlockSpec((1, t, heads * dv), lambda i, j, n: (i, n, j)),
        out_shape=jax.ShapeDtypeStruct((b, s, h * dv), BF16),
        compiler_params=_params("arbitrary", "arbitrary", "arbitrary"),
        scratch_shapes=(
            [pltpu.VMEM((heads, 1, t), F32),
             pltpu.VMEM((heads, 1, t), F32),
             pltpu.VMEM((heads, dv, t), F32)]
            + [pltpu.VMEM((2, t, t), F32)] * heads
            + [pltpu.VMEM((2, 1, t), F32)] * heads),
        name="flash_attention",
    )(q, k, vt)


def _route_rows(logits, bias_col):
    scores = jax.nn.sigmoid(logits)
    biased = scores + bias_col
    b_rows = [biased[i:i + 1, :] for i in range(N_EXPERTS)]
    s_rows = [scores[i:i + 1, :] for i in range(N_EXPERTS)]
    in_top = []
    group_score = []
    for g in range(N_GROUPS):
        grp = b_rows[g * EXPERTS_PER_GROUP:(g + 1) * EXPERTS_PER_GROUP]
        gs = None
        for i in range(EXPERTS_PER_GROUP):
            rank = None
            for j in range(EXPERTS_PER_GROUP):
                if j == i:
                    continue
                beats = (grp[j] >= grp[i]) if j < i else (grp[j] > grp[i])
                cnt = jnp.where(beats, 1.0, 0.0)
                rank = cnt if rank is None else rank + cnt
            sel = rank < float(TOP_K)
            in_top.append(sel)
            term = jnp.where(sel, grp[i], 0.0)
            gs = term if gs is None else gs + term
        group_score.append(gs)
    picked = []
    for g in range(N_GROUPS):
        lost = None
        for g2 in range(N_GROUPS):
            if g2 == g:
                continue
            beats = (group_score[g2] >= group_score[g]) if g2 < g else (group_score[g2] > group_score[g])
            cnt = jnp.where(beats, 1.0, 0.0)
            lost = cnt if lost is None else lost + cnt
        picked.append(lost < 1.0)
    e_lo = e_hi = w_lo = w_hi = denom = None
    for i in range(N_EXPERTS):
        sel = jnp.where(picked[i // EXPERTS_PER_GROUP], jnp.where(in_top[i], 1.0, 0.0), 0.0)
        w = sel * s_rows[i]
        if i == 0:
            seen = sel
            e_lo, e_hi, w_lo, w_hi, denom = jnp.zeros_like(w), jnp.zeros_like(w), w, jnp.zeros_like(w), w
        else:
            first = sel * (1.0 - seen)
            second = sel * seen
            e_lo = e_lo + first * float(i)
            e_hi = e_hi + second * float(i)
            w_lo = w_lo + first * s_rows[i]
            w_hi = w_hi + second * s_rows[i]
            denom = denom + w
            seen = jnp.maximum(seen, sel)
    inv = 1.0 / denom
    pad = jnp.zeros((ROUTE_ROWS - 4,) + e_lo.shape[1:], F32)
    return jnp.concatenate([e_lo, e_hi, w_lo * inv, w_hi * inv, pad], axis=0)


def _split_bf16(x):
    hi = x.astype(BF16)
    lo = (x - hi.astype(F32)).astype(BF16)
    return hi, lo


def _pack_bf16_pairs(x):
    half = x.shape[-1] // 2
    bits = lax.bitcast_convert_type(x.astype(BF16).astype(F32), jnp.uint32)
    return (bits[:, :half] >> 16) | (bits[:, half:] & jnp.uint32(0xFFFF0000))


def _unpack_bf16_pairs(words):
    lo = lax.bitcast_convert_type(words << 16, F32)
    hi = lax.bitcast_convert_type(words & jnp.uint32(0xFFFF0000), F32)
    return jnp.concatenate([lo, hi], axis=-1)


def _mixer_tail(x, y, mod_ref, n2g, rw_ref, rb_ref, x_out, h_out, g_out):
    x1 = x + mod_ref[0, 2:3, :] * y
    x_out[0] = x1
    h2 = _modulated_norm(x1, n2g, mod_ref[0, 4:5, :], mod_ref[0, 3:4, :])
    h_hi, h_lo = _split_bf16(h2)
    half = h2.shape[-1] // 2
    h_out[0, :, 0:half] = _pack_bf16_pairs(h2)
    nt = (((1,), (1,)), ((), ()))
    a = lax.dot_general(rw_ref[...], h_hi, nt, preferred_element_type=F32)
    c = lax.dot_general(rw_ref[0:N_EXPERTS, :], h_lo, nt, preferred_element_type=F32)
    logits = a[0:N_EXPERTS, :] + a[N_EXPERTS:, :] + c
    route = _route_rows(logits, rb_ref[...])
    g_out[...] = route
    pad = jnp.zeros((LANES - ROUTE_ROWS, route.shape[1]), F32)
    meta = jnp.concatenate([route, pad], axis=0).T
    h_out[0, :, half:half + LANES] = lax.bitcast_convert_type(meta, jnp.uint32)


def _attn_out_kernel(o_ref, x_ref, mod_ref, wo_ref, n2g_ref, rw_ref, rb_ref, x_out, h_out, g_out):
    y = jnp.dot(o_ref[0], wo_ref[...], preferred_element_type=F32)
    _mixer_tail(x_ref[0], y, mod_ref, n2g_ref[...], rw_ref, rb_ref, x_out, h_out, g_out)


def _tail_out_specs(b, s, d, tm):
    nt = s // tm
    specs = [
        pl.BlockSpec((1, tm, d), lambda i, j: (i, j, 0)),
        pl.BlockSpec((1, tm, d // 2 + LANES), lambda i, j: (i, j, 0)),
        pl.BlockSpec((ROUTE_ROWS, tm), lambda i, j: (0, i * nt + j)),
    ]
    shapes = [
        jax.ShapeDtypeStruct((b, s, d), F32),
        jax.ShapeDtypeStruct((b, s, d // 2 + LANES), jnp.uint32),
        jax.ShapeDtypeStruct((ROUTE_ROWS, b * s), F32),
    ]
    return specs, shapes


def _attn_out(o, x, mod, wo, n2g, rw, rb, tm=512):
    b, s, d = x.shape
    const = lambda i, j: (0, 0)
    out_specs, out_shape = _tail_out_specs(b, s, d, tm)
    return pl.pallas_call(
        _attn_out_kernel,
        grid=(b, s // tm),
        in_specs=[
            pl.BlockSpec((1, tm, o.shape[-1]), lambda i, j: (i, j, 0)),
            pl.BlockSpec((1, tm, d), lambda i, j: (i, j, 0)),
            pl.BlockSpec((1, 6, d), lambda i, j: (i, 0, 0)),
            pl.BlockSpec(wo.shape, const),
            pl.BlockSpec((1, d), const),
            pl.BlockSpec(rw.shape, const),
            pl.BlockSpec((N_EXPERTS, 1), const),
        ],
        out_specs=out_specs,
        out_shape=out_shape,
        compiler_params=_params("arbitrary", "arbitrary"),
        name="attn_out_router",
    )(o, x, mod, wo, n2g.reshape(1, d), rw, rb)


def _conv_mixer_kernel(x_ref, mod_ref, n1g_ref, w1_ref, b1_ref, dww_ref, dwb_ref, lng_ref, lnb_ref,
                       w2_ref, b2_ref, n2g_ref, rw_ref, rb_ref, x_out, h_out, g_out, ubuf, vbuf, cbuf,
                       *, ts, rows):
    j = pl.program_id(1)
    x = x_ref[0]
    d = x.shape[-1]
    h = _modulated_norm(x, n1g_ref[...], mod_ref[0, 1:2, :], mod_ref[0, 0:1, :]).astype(BF16)
    u = jnp.dot(h, w1_ref[...], preferred_element_type=F32) + b1_ref[...]
    u = u[:, :d] * jax.nn.sigmoid(u[:, d:])

    nc = d // LANES

    @pl.when(j == 0)
    def _():
        ubuf[:, 0:CONV_HALO, :] = jnp.zeros((nc, CONV_HALO, LANES), F32)

    @pl.when(j > 0)
    def _():
        ubuf[:, 0:CONV_HALO, :] = ubuf[:, ts:ts + CONV_HALO, :]

    for cc in range(nc):
        ubuf[cc, CONV_HALO:CONV_HALO + ts, :] = u[:, cc * LANES:(cc + 1) * LANES]

    first = CONV_HALO - (CONV_WIDTH - 1)

    def conv_chunk(cc, carry):
        for r in range(8):
            span = ts + 8 * ((CONV_WIDTH - 1 - r) // 8)
            vbuf[r, 0:span, :] = ubuf[cc, first + r:first + r + span, :]
        for i in range(ts // rows):
            acc = None
            for tap in range(CONV_WIDTH):
                r0 = i * rows + 8 * (tap // 8)
                term = dww_ref[cc, tap:tap + 1, :] * vbuf[tap % 8, r0:r0 + rows, :]
                acc = term if acc is None else acc + term
            cbuf[cc, i * rows:(i + 1) * rows, :] = acc
        return carry

    lax.fori_loop(0, nc, conv_chunk, 0)
    v = jnp.concatenate([cbuf[cc] for cc in range(nc)], axis=-1) + dwb_ref[...]
    mu = jnp.mean(v, axis=-1, keepdims=True)
    vc = v - mu
    var = jnp.mean(vc * vc, axis=-1, keepdims=True)
    v = _silu(vc * lax.rsqrt(var + NORM_EPS) * lng_ref[...] + lnb_ref[...])
    y = jnp.dot(v.astype(BF16), w2_ref[...], preferred_element_type=F32) + b2_ref[...]
    _mixer_tail(x, y, mod_ref, n2g_ref[...], rw_ref, rb_ref, x_out, h_out, g_out)


def _conv_mixer(x, mod, n1g, w1, b1, dww, dwb, lng, lnb, w2, b2, n2g, rw, rb, ts=512, rows=64):
    b, s, d = x.shape
    nc = d // LANES
    const = lambda i, j: (0, 0)
    row = lambda a: a.reshape(1, -1)
    dww = dww.reshape(CONV_WIDTH, nc, LANES).transpose(1, 0, 2)
    out_specs, out_shape = _tail_out_specs(b, s, d, ts)
    return pl.pallas_call(
        functools.partial(_conv_mixer_kernel, ts=ts, rows=rows),
        grid=(b, s // ts),
        in_specs=[
            pl.BlockSpec((1, ts, d), lambda i, j: (i, j, 0)),
            pl.BlockSpec((1, 6, d), lambda i, j: (i, 0, 0)),
            pl.BlockSpec((1, d), const),
            pl.BlockSpec(w1.shape, const),
            pl.BlockSpec((1, 2 * d), const),
            pl.BlockSpec(dww.shape, lambda i, j: (0, 0, 0)),
            pl.BlockSpec((1, d), const),
            pl.BlockSpec((1, d), const),
            pl.BlockSpec((1, d), const),
            pl.BlockSpec(w2.shape, const),
            pl.BlockSpec((1, d), const),
            pl.BlockSpec((1, d), const),
            pl.BlockSpec(rw.shape, const),
            pl.BlockSpec((N_EXPERTS, 1), const),
        ],
        out_specs=out_specs,
        out_shape=out_shape,
        scratch_shapes=[pltpu.VMEM((nc, ts + CONV_HALO, LANES), F32),
                        pltpu.VMEM((8, ts + CONV_HALO, LANES), F32),
                        pltpu.VMEM((nc, ts, LANES), F32)],
        compiler_params=_params("arbitrary", "arbitrary"),
        name="conv_mixer_router",
    )(x, mod, row(n1g), w1, row(b1), dww, row(dwb), row(lng), row(lnb), w2, row(b2), row(n2g), rw, rb)


MOE_TILE = 512
PAIRS_PER_GROUP = EXPERTS_PER_GROUP * (EXPERTS_PER_GROUP - 1) // 2
PAIR_CLASSES = N_GROUPS * PAIRS_PER_GROUP


def _class_experts():
    lo, hi = [], []
    for g in range(N_GROUPS):
        for a in range(EXPERTS_PER_GROUP):
            for b in range(a + 1, EXPERTS_PER_GROUP):
                lo.append(g * EXPERTS_PER_GROUP + a)
                hi.append(g * EXPERTS_PER_GROUP + b)
    return jnp.asarray(lo, jnp.int32), jnp.asarray(hi, jnp.int32)


def _dispatch_plan(route, tm):
    t = route.shape[1]
    n_tiles_max = t // tm + PAIR_CLASSES
    i32 = jnp.int32
    e_lo, e_hi = route[0].astype(i32), route[1].astype(i32)
    a, b = e_lo % EXPERTS_PER_GROUP, e_hi % EXPERTS_PER_GROUP
    pair = a * (2 * EXPERTS_PER_GROUP - 1 - a) // 2 + (b - a - 1)
    cls = (e_lo // EXPERTS_PER_GROUP) * PAIRS_PER_GROUP + pair
    onehot = (cls[:, None] == jnp.arange(PAIR_CLASSES, dtype=i32)).astype(i32)
    blocks = onehot.reshape(t // tm, tm, PAIR_CLASSES).astype(F32)
    within = jnp.einsum("ij,bjc->bic", jnp.tril(jnp.ones((tm, tm), F32)), blocks)
    totals = within[:, -1, :]
    before = jnp.cumsum(totals, axis=0) - totals
    csum = (within + before[:, None, :]).reshape(t, PAIR_CLASSES).astype(i32)
    cnt = csum[-1]
    padded = (cnt + tm - 1) // tm * tm
    pad_end = jnp.cumsum(padded)
    off = pad_end - padded
    pos = jnp.sum(onehot * (csum - onehot + off[None, :]), axis=-1)
    n_tiles = (pad_end[-1] // tm).astype(i32)
    tile = jnp.minimum(jnp.arange(n_tiles_max, dtype=i32), n_tiles - 1)
    tile_class = jnp.searchsorted(pad_end // tm, tile, side="right").astype(i32)
    class_lo, class_hi = _class_experts()
    return pos.reshape(t // tm, 1, tm), class_lo[tile_class], class_hi[tile_class], n_tiles.reshape(1)


def _gather_pipeline_step(step, n_steps, idx_hbm, src_hbm, idx_smem, buf, isem, gsem, rows):
    n_steps = jnp.asarray(n_steps, jnp.int32)

    def idx_copy(s, sl):
        return pltpu.make_async_copy(idx_hbm.at[s], idx_smem.at[sl], isem.at[sl])

    def start_rows(sl):
        def body(g, carry):
            base = pl.multiple_of(g * SUBLANES, SUBLANES)
            for j in range(SUBLANES):
                row = idx_smem[sl, 0, base + j]
                pltpu.make_async_copy(src_hbm.at[row >> 3, pl.ds(row & (SUBLANES - 1), 1)],
                                      buf.at[sl, g, pl.ds(j, 1)], gsem.at[sl]).start()
            return carry

        lax.fori_loop(0, rows // SUBLANES, body, 0)

    def pipeline(slot):
        nslot = 1 - slot

        @pl.when(step == 0)
        def _():
            first = idx_copy(0, 0)
            first.start()
            first.wait()
            start_rows(0)

            @pl.when(n_steps > 1)
            def _():
                idx_copy(1, 1).start()

        @pl.when(step + 1 < n_steps)
        def _():
            idx_copy(step + 1, nslot).wait()
            start_rows(nslot)

            @pl.when(step + 2 < n_steps)
            def _():
                idx_copy(step + 2, slot).start()

        @pl.when(step < n_steps)
        def _():
            pltpu.make_async_copy(src_hbm.at[pl.ds(0, rows // SUBLANES)], buf.at[slot], gsem.at[slot]).wait()

    for parity in range(2):
        pl.when(step % 2 == parity)(functools.partial(pipeline, parity))


def _gather_scratch(rows, words):
    return [pltpu.SMEM((2, 1, rows), jnp.int32), pltpu.VMEM((2, rows // SUBLANES, SUBLANES, words), jnp.uint32),
            pltpu.SemaphoreType.DMA((2,)), pltpu.SemaphoreType.DMA((2,))]


def _moe_dispatch_kernel(pos_hbm, h_ref, xs_zero_hbm, xs_hbm, idx_smem, hbuf, isem, ssem, *, tm, n_steps):
    del xs_zero_hbm
    i = pl.program_id(0) * pl.num_programs(1) + pl.program_id(1)
    groups = tm // SUBLANES

    def idx_copy(s, sl):
        return pltpu.make_async_copy(pos_hbm.at[s], idx_smem.at[sl], isem.at[sl])

    def wait_rows(sl):
        pltpu.make_async_copy(hbuf.at[sl], xs_hbm.at[pl.ds(0, groups)], ssem.at[sl]).wait()

    def step(slot):
        nslot = 1 - slot

        @pl.when(i == 0)
        def _():
            idx_copy(0, 0).start()

        idx_copy(i, slot).wait()

        @pl.when(i + 1 < n_steps)
        def _():
            idx_copy(i + 1, nslot).start()

        hbuf[slot] = h_ref[0].reshape(groups, SUBLANES, -1)

        def body(g, carry):
            base = pl.multiple_of(g * SUBLANES, SUBLANES)
            for j in range(SUBLANES):
                row = idx_smem[slot, 0, base + j]
                pltpu.make_async_copy(hbuf.at[slot, g, pl.ds(j, 1)],
                                      xs_hbm.at[row >> 3, pl.ds(row & (SUBLANES - 1), 1)],
                                      ssem.at[slot]).start()
            return carry

        lax.fori_loop(0, groups, body, 0)

        @pl.when(i > 0)
        def _():
            wait_rows(nslot)

        @pl.when(i == n_steps - 1)
        def _():
            wait_rows(slot)

    for parity in range(2):
        pl.when(i % 2 == parity)(functools.partial(step, parity))


def _moe_dispatch(pos_tiles, h_words, n_tiles_max, tm):
    b, s, words = h_words.shape
    groups = tm // SUBLANES
    xs_shape = (n_tiles_max * groups, SUBLANES, words)
    return pl.pallas_call(
        functools.partial(_moe_dispatch_kernel, tm=tm, n_steps=b * s // tm),
        grid=(b, s // tm),
        in_specs=[
            pl.BlockSpec(memory_space=pl.ANY),
            pl.BlockSpec((1, tm, words), lambda i, j: (i, j, 0)),
            pl.BlockSpec(memory_space=pl.ANY),
        ],
        out_specs=pl.BlockSpec(memory_space=pl.ANY),
        out_shape=jax.ShapeDtypeStruct(xs_shape, jnp.uint32),
        input_output_aliases={2: 0},
        scratch_shapes=[pltpu.SMEM((2, 1, tm), jnp.int32),
                        pltpu.VMEM((2, groups, SUBLANES, words), jnp.uint32),
                        pltpu.SemaphoreType.DMA((2,)), pltpu.SemaphoreType.DMA((2,))],
        compiler_params=_params("arbitrary", "arbitrary"),
        name="moe_dispatch",
    )(pos_tiles, h_words, jnp.zeros(xs_shape, jnp.uint32))


def _moe_expert_kernel(lo_ref, hi_ref, nt_ref, x_ref, wgu_lo_ref, wgu_hi_ref, wd_lo_ref, wd_hi_ref, y_ref):
    del lo_ref, hi_ref
    i = pl.program_id(0)
    ff = wd_lo_ref.shape[1]
    half = wgu_lo_ref.shape[1] // 2

    @pl.when(i < nt_ref[0])
    def _():
        words = x_ref[...]
        x = _unpack_bf16_pairs(words[:, :half]).astype(BF16)
        meta = lax.bitcast_convert_type(words[:, half:], F32)
        y = None
        for slot, (wgu_ref, wd_ref) in enumerate(((wgu_lo_ref, wd_lo_ref), (wgu_hi_ref, wd_hi_ref))):
            gu = jnp.dot(x, wgu_ref[0], preferred_element_type=F32)
            a = _silu(gu[:, :ff]) * gu[:, ff:] * meta[:, 2 + slot:3 + slot]
            part = jnp.dot(a.astype(BF16), wd_ref[0], preferred_element_type=F32)
            y = part if y is None else y + part
        y_ref[...] = _pack_bf16_pairs(y)

    @pl.when(i >= nt_ref[0])
    def _():
        y_ref[...] = jnp.zeros(y_ref.shape, jnp.uint32)


def _moe_experts(tile_lo, tile_hi, n_tiles, xs, wgu, wd, tm):
    n_tiles_max = tile_lo.shape[0]
    words = xs.shape[1]
    d, ff = wgu.shape[1], wd.shape[1]
    grid_spec = pltpu.PrefetchScalarGridSpec(
        num_scalar_prefetch=3,
        grid=(n_tiles_max,),
        in_specs=[
            pl.BlockSpec((tm, words), lambda i, lo, hi, nt: (i, 0)),
            pl.BlockSpec((1, d, 2 * ff), lambda i, lo, hi, nt: (lo[i], 0, 0)),
            pl.BlockSpec((1, d, 2 * ff), lambda i, lo, hi, nt: (hi[i], 0, 0)),
            pl.BlockSpec((1, ff, d), lambda i, lo, hi, nt: (lo[i], 0, 0)),
            pl.BlockSpec((1, ff, d), lambda i, lo, hi, nt: (hi[i], 0, 0)),
        ],
        out_specs=pl.BlockSpec((tm, d // 2), lambda i, lo, hi, nt: (i, 0)),
    )
    return pl.pallas_call(
        _moe_expert_kernel,
        grid_spec=grid_spec,
        out_shape=jax.ShapeDtypeStruct((n_tiles_max * tm, d // 2), jnp.uint32),
        compiler_params=_params("arbitrary"),
        name="moe_experts",
    )(tile_lo, tile_hi, n_tiles, xs, wgu, wgu, wd, wd)


def _moe_combine_kernel(pos_hbm, y_hbm, x_ref, mod_ref, o_ref, idx_smem, ybuf, isem, gsem, *, tm, n_steps):
    i = pl.program_id(0) * pl.num_programs(1) + pl.program_id(1)
    _gather_pipeline_step(i, n_steps, pos_hbm, y_hbm, idx_smem, ybuf, isem, gsem, tm)
    o_ref[0] = x_ref[0] + mod_ref[0, 5:6, :] * _unpack_bf16_pairs(ybuf[i % 2].reshape(tm, -1))


def _moe_combine(pos_tiles, y_words, x, mod, tm):
    b, s, d = x.shape
    n_steps = b * s // tm
    words = y_words.shape[1]
    return pl.pallas_call(
        functools.partial(_moe_combine_kernel, tm=tm, n_steps=n_steps),
        grid=(b, s // tm),
        in_specs=[
            pl.BlockSpec(memory_space=pl.ANY),
            pl.BlockSpec(memory_space=pl.ANY),
            pl.BlockSpec((1, tm, d), lambda i, j: (i, j, 0)),
            pl.BlockSpec((1, 6, d), lambda i, j: (i, 0, 0)),
        ],
        out_specs=pl.BlockSpec((1, tm, d), lambda i, j: (i, j, 0)),
        out_shape=jax.ShapeDtypeStruct((b, s, d), F32),
        scratch_shapes=_gather_scratch(tm, words),
        compiler_params=_params("arbitrary", "arbitrary"),
        name="moe_combine",
    )(pos_tiles, y_words.reshape(-1, SUBLANES, words), x, mod)


def _moe_layer(h_words, route, x1, mod, w_gate, w_up, w_down):
    pos_tiles, tile_lo, tile_hi, n_tiles = _dispatch_plan(route, MOE_TILE)
    wgu = jnp.concatenate([w_gate, w_up], axis=-1).astype(BF16)
    xs = _moe_dispatch(pos_tiles, h_words, tile_lo.shape[0], MOE_TILE)
    xs = xs.reshape(-1, xs.shape[-1])
    y_words = _moe_experts(tile_lo, tile_hi, n_tiles, xs, wgu, w_down.astype(BF16), MOE_TILE)
    return _moe_combine(pos_tiles, y_words, x1, mod, MOE_TILE)


def kernel(x, c, positions, ada_w, ada_b, norm1_g, norm2_g, mla_w_dqkv, mla_q_norm_g, mla_w_uq, mla_kv_norm_g, mla_w_ukv, mla_qk_q_g, mla_qk_k_g, mla_w_o, conv_pw1_w, conv_pw1_b, conv_dw_w, conv_dw_b, conv_ln_g, conv_ln_b, conv_pw2_w, conv_pw2_b, router_w, router_bias, moe_w_gate, moe_w_up, moe_w_down):
    b, s, d = x.shape
    depth = ada_w.shape[0]
    mod_all = _adaln(c, ada_w, ada_b).reshape(depth, b, 6, d)

    rw_t = router_w.astype(F32).T
    rw_hi = rw_t.astype(BF16)
    rw_lo = (rw_t - rw_hi.astype(F32)).astype(BF16)
    rw = jnp.concatenate([rw_hi, rw_lo], axis=0)
    rb = router_bias.astype(F32).reshape(N_EXPERTS, 1)

    for i in range(depth):
        mod = mod_all[i]
        jdx = i // 2
        if i % 2 == 0:
            w_uq = mla_w_uq[jdx].reshape(Q_LORA, N_HEADS, QK_HEAD)
            wuq = jnp.concatenate([w_uq[:, :, :QK_NOPE].reshape(Q_LORA, -1),
                                   w_uq[:, :, QK_NOPE:].reshape(Q_LORA, -1)], axis=-1).astype(BF16)
            w_ukv = mla_w_ukv[jdx].reshape(KV_LORA, N_HEADS, QK_NOPE + V_HEAD)
            wuk = w_ukv[:, :, :QK_NOPE].reshape(KV_LORA, -1).astype(BF16)
            wuvt = w_ukv[:, :, QK_NOPE:].reshape(KV_LORA, -1).T.astype(BF16)
            q, k, vt = _mla_proj(x, mod, norm1_g[i], mla_w_dqkv[jdx].astype(BF16), mla_q_norm_g[jdx], wuq,
                                 mla_kv_norm_g[jdx], wuk, wuvt, mla_qk_q_g[jdx], mla_qk_k_g[jdx], positions,
                                 ts=PROJ_TILE, vt_tile=ATTN_TILE)
            o = _flash_attention(q, k, vt, t=ATTN_TILE)
            x1, h2, gates_rows = _attn_out(o, x, mod, mla_w_o[jdx].astype(BF16), norm2_g[i], rw, rb)
        else:
            x1, h2, gates_rows = _conv_mixer(
                x, mod, norm1_g[i], conv_pw1_w[jdx].astype(BF16), conv_pw1_b[jdx], conv_dw_w[jdx],
                conv_dw_b[jdx], conv_ln_g[jdx], conv_ln_b[jdx], conv_pw2_w[jdx].astype(BF16),
                conv_pw2_b[jdx], norm2_g[i], rw, rb)
        x = _moe_layer(h2, gates_rows, x1, mod, moe_w_gate[i], moe_w_up[i], moe_w_down[i])
    return x
```

```python
import functools

import jax
import jax.numpy as jnp
from jax import lax
from jax.experimental import pallas as pl
from jax.experimental.pallas import tpu as pltpu

F32 = jnp.float32
BF16 = jnp.bfloat16

N_HEADS = 8
QK_NOPE = 128
QK_ROPE = 64
QK_HEAD = QK_NOPE + QK_ROPE
V_HEAD = 128
Q_LORA = 768
KV_LORA = 256
ROPE_THETA = 10000.0
CONV_WIDTH = 31
CONV_HALO = 32
N_EXPERTS = 16
N_GROUPS = 4
EXPERTS_PER_GROUP = N_EXPERTS // N_GROUPS
TOP_K = 2
ROUTE_ROWS = 8
NORM_EPS = 1e-6
LOG2_E = 1.4426950408889634
LANES = 128
SUBLANES = 8
PROJ_TILE = 256
ATTN_TILE = 512
VMEM_LIMIT_BYTES = 52 * 1024 * 1024


def _params(*semantics):
    return pltpu.CompilerParams(dimension_semantics=semantics, vmem_limit_bytes=VMEM_LIMIT_BYTES)


def _rms(x):
    return x * lax.rsqrt(jnp.mean(x * x, axis=-1, keepdims=True) + NORM_EPS)


def _silu(x):
    return x * jax.nn.sigmoid(x)


def _modulated_norm(x, g, scale, shift):
    return (_rms(x) * g) * (1.0 + scale) + shift


def _adaln_kernel(c_ref, w_ref, b_ref, o_ref):
    a = _silu(c_ref[...]).astype(BF16)
    o_ref[0] = jnp.dot(a, w_ref[0].astype(BF16), preferred_element_type=F32) + b_ref[0]


def _adaln(c, ada_w, ada_b, tn=1536):
    depth, d, n = ada_w.shape
    b = c.shape[0]
    return pl.pallas_call(
        _adaln_kernel,
        grid=(depth, n // tn),
        in_specs=[
            pl.BlockSpec((b, d), lambda l, j: (0, 0)),
            pl.BlockSpec((1, d, tn), lambda l, j: (l, 0, j)),
            pl.BlockSpec((1, 1, tn), lambda l, j: (l, 0, j)),
        ],
        out_specs=pl.BlockSpec((1, b, tn), lambda l, j: (l, 0, j)),
        out_shape=jax.ShapeDtypeStruct((depth, b, n), F32),
        compiler_params=_params("arbitrary", "arbitrary"),
        name="adaln",
    )(c, ada_w, ada_b.reshape(depth, 1, n))


def _rope128(x, cos, sin_signed, lo_mask):
    swapped = jnp.where(lo_mask, pltpu.roll(x, LANES - QK_ROPE // 2, 1), pltpu.roll(x, QK_ROPE // 2, 1))
    return x * cos + swapped * sin_signed


def _mla_proj_kernel(x_ref, mod_ref, n1g_ref, wd_ref, qg_ref, wuq_ref, kvg_ref, wuk_ref, wuvt_ref,
                     gq_ref, gk_ref, pos_ref, invf_ref, q_out, k_out, vt_out):
    x = x_ref[0]
    h = _modulated_norm(x, n1g_ref[...], mod_ref[0, 1:2, :], mod_ref[0, 0:1, :]).astype(BF16)
    lat = jnp.dot(h, wd_ref[...], preferred_element_type=F32)
    c_q = (_rms(lat[:, :Q_LORA]) * qg_ref[...]).astype(BF16)
    c_kv = (_rms(lat[:, Q_LORA:Q_LORA + KV_LORA]) * kvg_ref[...]).astype(BF16)
    k_pe = lat[:, Q_LORA + KV_LORA:]
    q = jnp.dot(c_q, wuq_ref[...], preferred_element_type=F32)
    k_nope = jnp.dot(c_kv, wuk_ref[...], preferred_element_type=F32)
    v_t = lax.dot_general(wuvt_ref[...], c_kv, (((1,), (1,)), ((), ())), preferred_element_type=F32)

    ang = pos_ref[0].astype(F32) * invf_ref[...]
    cos = jnp.cos(ang)
    lane = lax.broadcasted_iota(jnp.int32, (1, LANES), 1)
    lo_mask = (lane % QK_ROPE) < (QK_ROPE // 2)
    sin_signed = jnp.where(lo_mask, -1.0, 1.0) * jnp.sin(ang)

    gq = gq_ref[...]
    gk = gk_ref[...]
    gq_rope2 = jnp.concatenate([gq[:, QK_NOPE:], gq[:, QK_NOPE:]], axis=-1)
    gk_rope2 = jnp.concatenate([gk[:, QK_NOPE:], gk[:, QK_NOPE:]], axis=-1)
    sm_scale = QK_HEAD ** -0.5 * LOG2_E
    nope_all = N_HEADS * QK_NOPE

    k_pe2 = jnp.concatenate([k_pe, k_pe], axis=-1)
    k_rope = _rope128(k_pe2 * gk_rope2, cos, sin_signed, lo_mask)[:, :QK_ROPE]
    ss_pe = jnp.sum(k_pe * k_pe, axis=-1, keepdims=True)

    for pair in range(N_HEADS // 2):
        q_rope_pair = q[:, nope_all + pair * LANES: nope_all + (pair + 1) * LANES]
        q_roped_pair = _rope128(q_rope_pair * gq_rope2, cos, sin_signed, lo_mask)
        for sub in range(2):
            hd = 2 * pair + sub
            qn = q[:, hd * QK_NOPE:(hd + 1) * QK_NOPE]
            qr = q_rope_pair[:, sub * QK_ROPE:(sub + 1) * QK_ROPE]
            ss = jnp.sum(qn * qn, axis=-1, keepdims=True) + jnp.sum(qr * qr, axis=-1, keepdims=True)
            r = lax.rsqrt(ss * (1.0 / QK_HEAD) + NORM_EPS) * sm_scale
            q_out[0, hd, :, 0:QK_NOPE] = (qn * r * gq[:, :QK_NOPE]).astype(BF16)
            q_out[0, hd, :, QK_NOPE:QK_HEAD] = (
                q_roped_pair[:, sub * QK_ROPE:(sub + 1) * QK_ROPE] * r).astype(BF16)

            kn = k_nope[:, hd * QK_NOPE:(hd + 1) * QK_NOPE]
            rk = lax.rsqrt((jnp.sum(kn * kn, axis=-1, keepdims=True) + ss_pe) * (1.0 / QK_HEAD) + NORM_EPS)
            k_out[0, hd, :, 0:QK_NOPE] = (kn * rk * gk[:, :QK_NOPE]).astype(BF16)
            k_out[0, hd, :, QK_NOPE:QK_HEAD] = (k_rope * rk).astype(BF16)
            vt_out[0, hd, 0] = v_t[hd * V_HEAD:(hd + 1) * V_HEAD, :].astype(BF16)


def _mla_proj(x, mod, n1g, wd, qg, wuq, kvg, wuk, wuvt, gq, gk, positions, ts, vt_tile):
    b, s, d = x.shape
    per = vt_tile // ts
    half = QK_ROPE // 2
    inv_freq = 1.0 / (ROPE_THETA ** (jnp.arange(half, dtype=F32) * (2.0 / QK_ROPE)))
    invf = jnp.tile(inv_freq, LANES // half).reshape(1, LANES)
    const = lambda i, j: (0, 0)
    return pl.pallas_call(
        _mla_proj_kernel,
        grid=(b, s // ts),
        in_specs=[
            pl.BlockSpec((1, ts, d), lambda i, j: (i, j, 0)),
            pl.BlockSpec((1, 6, d), lambda i, j: (i, 0, 0)),
            pl.BlockSpec((1, d), const),
            pl.BlockSpec(wd.shape, const),
            pl.BlockSpec((1, Q_LORA), const),
            pl.BlockSpec(wuq.shape, const),
            pl.BlockSpec((1, KV_LORA), const),
            pl.BlockSpec(wuk.shape, const),
            pl.BlockSpec(wuvt.shape, const),
            pl.BlockSpec((1, QK_HEAD), const),
            pl.BlockSpec((1, QK_HEAD), const),
            pl.BlockSpec((1, ts, 1), lambda i, j: (i, j, 0)),
            pl.BlockSpec((1, LANES), const),
        ],
        out_specs=[
            pl.BlockSpec((1, N_HEADS, ts, QK_HEAD), lambda i, j: (i, 0, j, 0)),
            pl.BlockSpec((1, N_HEADS, ts, QK_HEAD), lambda i, j: (i, 0, j, 0)),
            pl.BlockSpec((1, N_HEADS, 1, V_HEAD, ts), lambda i, j: (i, 0, j // per, 0, j % per)),
        ],
        out_shape=[
            jax.ShapeDtypeStruct((b, N_HEADS, s, QK_HEAD), BF16),
            jax.ShapeDtypeStruct((b, N_HEADS, s, QK_HEAD), BF16),
            jax.ShapeDtypeStruct((b, N_HEADS, s // vt_tile, V_HEAD, vt_tile), BF16),
        ],
        compiler_params=_params("arbitrary", "arbitrary"),
        name="mla_proj",
    )(x, mod, n1g.reshape(1, d), wd, qg.reshape(1, -1), wuq, kvg.reshape(1, -1), wuk, wuvt,
      gq.reshape(1, -1), gk.reshape(1, -1), positions.reshape(b, s, 1), invf)


def _flash_kernel(q_ref, k_ref, vt_ref, o_ref, m_ref, l_ref, acc_ref, *stage_refs, t):
    qi = pl.program_id(2)
    heads = q_ref.shape[1]
    s_refs, mb_refs = stage_refs[:heads], stage_refs[heads:]
    nt = (((1,), (1,)), ((), ()))

    def scores(kb, hh, masked, slot=None):
        slot = kb % 2 if slot is None else slot
        start = pl.multiple_of(kb * t, t)
        s = lax.dot_general(k_ref[0, hh, pl.ds(start, t), :], q_ref[0, hh], nt,
                            preferred_element_type=F32)
        if masked:
            kv_pos = lax.broadcasted_iota(jnp.int32, (t, t), 0)
            q_pos = lax.broadcasted_iota(jnp.int32, (t, t), 1)
            s = jnp.where(kv_pos <= q_pos, s, -jnp.inf)
        s_refs[hh][slot] = s
        mb_refs[hh][slot] = jnp.max(s, axis=0, keepdims=True)

    def softmax_pv(kb, hh, slot=None):
        slot = kb % 2 if slot is None else slot
        m = m_ref[hh]
        m_new = jnp.maximum(m, mb_refs[hh][slot])
        alpha = jnp.exp2(m - m_new)
        p = jnp.exp2(s_refs[hh][slot] - m_new)
        l_ref[hh] = alpha * l_ref[hh] + jnp.sum(p, axis=0, keepdims=True)
        m_ref[hh] = m_new
        pv = jnp.dot(vt_ref[0, hh, kb], p.astype(BF16), preferred_element_type=F32)
        acc_ref[hh] = alpha * acc_ref[hh] + pv

    def stage(softmax_kb, scores_kb, masked, softmax_slot=None):
        scores_slot = None if softmax_slot is None else 1 - softmax_slot
        for hh in range(heads):
            if softmax_kb is not None:
                softmax_pv(softmax_kb, hh, softmax_slot)
            if scores_kb is not None:
                scores(scores_kb, hh, masked, scores_slot)

    m_ref[...] = jnp.full(m_ref.shape, -jnp.inf, F32)
    l_ref[...] = jnp.zeros(l_ref.shape, F32)
    acc_ref[...] = jnp.zeros(acc_ref.shape, F32)

    @pl.when(qi == 0)
    def _():
        stage(None, qi, True)

    @pl.when(qi > 0)
    def _():
        stage(None, 0, False)

        steady = qi - 1

        def body(pair, carry):
            kb = 2 * pair
            stage(kb, kb + 1, False, softmax_slot=0)
            stage(kb + 1, kb + 2, False, softmax_slot=1)
            return carry

        lax.fori_loop(0, steady // 2, body, 0)

        @pl.when(steady % 2 == 1)
        def _():
            stage(steady - 1, steady, False)

        stage(qi - 1, qi, True)

    stage(qi, None, False)
    for hh in range(heads):
        o_ref[0, :, hh * V_HEAD:(hh + 1) * V_HEAD] = (acc_ref[hh] * (1.0 / l_ref[hh])).T.astype(BF16)


def _flash_attention(q, k, vt, t, heads=4):
    b, h, s, dqk = q.shape
    dv = vt.shape[-2]
    assert vt.shape[-1] == t and h % heads == 0
    return pl.pallas_call(
        functools.partial(_flash_kernel, t=t),
        grid=(b, h // heads, s // t),
        in_specs=[
            pl.BlockSpec((1, heads, t, dqk), lambda i, j, n: (i, j, n, 0)),
            pl.BlockSpec((1, heads, s, dqk), lambda i, j, n: (i, j, 0, 0), pipeline_mode=pl.Buffered(1)),
            pl.BlockSpec((1, heads, s // t, dv, t), lambda i, j, n: (i, j, 0, 0, 0),
                         pipeline_mode=pl.Buffered(1)),
        ],
        out_specs=pl.BlockSpec((1, t, heads * dv), lambda i, j, n: (i, n, j)),
        out_shape=jax.ShapeDtypeStruct((b, s, h * dv), BF16),
        compiler_params=_params("arbitrary", "arbitrary", "arbitrary"),
        scratch_shapes=(
            [pltpu.VMEM((heads, 1, t), F32),
             pltpu.VMEM((heads, 1, t), F32),
             pltpu.VMEM((heads, dv, t), F32)]
            + [pltpu.VMEM((2, t, t), F32)] * heads
            + [pltpu.VMEM((2, 1, t), F32)] * heads),
        name="flash_attention",
    )(q, k, vt)


def _route_rows(logits, bias_col):
    scores = jax.nn.sigmoid(logits)
    biased = scores + bias_col
    b_rows = [biased[i:i + 1, :] for i in range(N_EXPERTS)]
    s_rows = [scores[i:i + 1, :] for i in range(N_EXPERTS)]
    in_top = []
    group_score = []
    for g in range(N_GROUPS):
        grp = b_rows[g * EXPERTS_PER_GROUP:(g + 1) * EXPERTS_PER_GROUP]
        gs = None
        for i in range(EXPERTS_PER_GROUP):
            rank = None
            for j in range(EXPERTS_PER_GROUP):
                if j == i:
                    continue
                beats = (grp[j] >= grp[i]) if j < i else (grp[j] > grp[i])
                cnt = jnp.where(beats, 1.0, 0.0)
                rank = cnt if rank is None else rank + cnt
            sel = rank < float(TOP_K)
            in_top.append(sel)
            term = jnp.where(sel, grp[i], 0.0)
            gs = term if gs is None else gs + term
        group_score.append(gs)
    picked = []
    for g in range(N_GROUPS):
        lost = None
        for g2 in range(N_GROUPS):
            if g2 == g:
                continue
            beats = (group_score[g2] >= group_score[g]) if g2 < g else (group_score[g2] > group_score[g])
            cnt = jnp.where(beats, 1.0, 0.0)
            lost = cnt if lost is None else lost + cnt
        picked.append(lost < 1.0)
    e_lo = e_hi = w_lo = w_hi = denom = None
    for i in range(N_EXPERTS):
        sel = jnp.where(picked[i // EXPERTS_PER_GROUP], jnp.where(in_top[i], 1.0, 0.0), 0.0)
        w = sel * s_rows[i]
        if i == 0:
            seen = sel
            e_lo, e_hi, w_lo, w_hi, denom = jnp.zeros_like(w), jnp.zeros_like(w), w, jnp.zeros_like(w), w
        else:
            first = sel * (1.0 - seen)
            second = sel * seen
            e_lo = e_lo + first * float(i)
            e_hi = e_hi + second * float(i)
            w_lo = w_lo + first * s_rows[i]
            w_hi = w_hi + second * s_rows[i]
            denom = denom + w
            seen = jnp.maximum(seen, sel)
    inv = 1.0 / denom
    pad = jnp.zeros((ROUTE_ROWS - 4,) + e_lo.shape[1:], F32)
    return jnp.concatenate([e_lo, e_hi, w_lo * inv, w_hi * inv, pad], axis=0)


def _split_bf16(x):
    hi = x.astype(BF16)
    lo = (x - hi.astype(F32)).astype(BF16)
    return hi, lo


def _pack_bf16_pairs(x):
    half = x.shape[-1] // 2
    bits = lax.bitcast_convert_type(x.astype(BF16).astype(F32), jnp.uint32)
    return (bits[:, :half] >> 16) | (bits[:, half:] & jnp.uint32(0xFFFF0000))


def _unpack_bf16_pairs(words):
    lo = lax.bitcast_convert_type(words << 16, F32)
    hi = lax.bitcast_convert_type(words & jnp.uint32(0xFFFF0000), F32)
    return jnp.concatenate([lo, hi], axis=-1)


def _mixer_tail(x, y, mod_ref, n2g, rw_ref, rb_ref, x_out, h_out, g_out):
    x1 = x + mod_ref[0, 2:3, :] * y
    x_out[0] = x1
    h2 = _modulated_norm(x1, n2g, mod_ref[0, 4:5, :], mod_ref[0, 3:4, :])
    h_hi, h_lo = _split_bf16(h2)
    half = h2.shape[-1] // 2
    h_out[0, :, 0:half] = _pack_bf16_pairs(h2)
    nt = (((1,), (1,)), ((), ()))
    a = lax.dot_general(rw_ref[...], h_hi, nt, preferred_element_type=F32)
    c = lax.dot_general(rw_ref[0:N_EXPERTS, :], h_lo, nt, preferred_element_type=F32)
    logits = a[0:N_EXPERTS, :] + a[N_EXPERTS:, :] + c
    route = _route_rows(logits, rb_ref[...])
    g_out[...] = route
    pad = jnp.zeros((LANES - ROUTE_ROWS, route.shape[1]), F32)
    meta = jnp.concatenate([route, pad], axis=0).T
    h_out[0, :, half:half + LANES] = lax.bitcast_convert_type(meta, jnp.uint32)


def _attn_out_kernel(o_ref, x_ref, mod_ref, wo_ref, n2g_ref, rw_ref, rb_ref, x_out, h_out, g_out):
    y = jnp.dot(o_ref[0], wo_ref[...], preferred_element_type=F32)
    _mixer_tail(x_ref[0], y, mod_ref, n2g_ref[...], rw_ref, rb_ref, x_out, h_out, g_out)


def _tail_out_specs(b, s, d, tm):
    nt = s // tm
    specs = [
        pl.BlockSpec((1, tm, d), lambda i, j: (i, j, 0)),
        pl.BlockSpec((1, tm, d // 2 + LANES), lambda i, j: (i, j, 0)),
        pl.BlockSpec((ROUTE_ROWS, tm), lambda i, j: (0, i * nt + j)),
    ]
    shapes = [
        jax.ShapeDtypeStruct((b, s, d), F32),
        jax.ShapeDtypeStruct((b, s, d // 2 + LANES), jnp.uint32),
        jax.ShapeDtypeStruct((ROUTE_ROWS, b * s), F32),
    ]
    return specs, shapes


def _attn_out(o, x, mod, wo, n2g, rw, rb, tm=512):
    b, s, d = x.shape
    const = lambda i, j: (0, 0)
    out_specs, out_shape = _tail_out_specs(b, s, d, tm)
    return pl.pallas_call(
        _attn_out_kernel,
        grid=(b, s // tm),
        in_specs=[
            pl.BlockSpec((1, tm, o.shape[-1]), lambda i, j: (i, j, 0)),
            pl.BlockSpec((1, tm, d), lambda i, j: (i, j, 0)),
            pl.BlockSpec((1, 6, d), lambda i, j: (i, 0, 0)),
            pl.BlockSpec(wo.shape, const),
            pl.BlockSpec((1, d), const),
            pl.BlockSpec(rw.shape, const),
            pl.BlockSpec((N_EXPERTS, 1), const),
        ],
        out_specs=out_specs,
        out_shape=out_shape,
        compiler_params=_params("arbitrary", "arbitrary"),
        name="attn_out_router",
    )(o, x, mod, wo, n2g.reshape(1, d), rw, rb)


def _conv_mixer_kernel(x_ref, mod_ref, n1g_ref, w1_ref, b1_ref, dww_ref, dwb_ref, lng_ref, lnb_ref,
                       w2_ref, b2_ref, n2g_ref, rw_ref, rb_ref, x_out, h_out, g_out, ubuf, vbuf, cbuf,
                       *, ts, rows):
    j = pl.program_id(1)
    x = x_ref[0]
    d = x.shape[-1]
    h = _modulated_norm(x, n1g_ref[...], mod_ref[0, 1:2, :], mod_ref[0, 0:1, :]).astype(BF16)
    u = jnp.dot(h, w1_ref[...], preferred_element_type=F32) + b1_ref[...]
    u = u[:, :d] * jax.nn.sigmoid(u[:, d:])

    nc = d // LANES

    @pl.when(j == 0)
    def _():
        ubuf[:, 0:CONV_HALO, :] = jnp.zeros((nc, CONV_HALO, LANES), F32)

    @pl.when(j > 0)
    def _():
        ubuf[:, 0:CONV_HALO, :] = ubuf[:, ts:ts + CONV_HALO, :]

    for cc in range(nc):
        ubuf[cc, CONV_HALO:CONV_HALO + ts, :] = u[:, cc * LANES:(cc + 1) * LANES]

    first = CONV_HALO - (CONV_WIDTH - 1)

    def conv_chunk(cc, carry):
        for r in range(8):
            span = ts + 8 * ((CONV_WIDTH - 1 - r) // 8)
            vbuf[r, 0:span, :] = ubuf[cc, first + r:first + r + span, :]
        for i in range(ts // rows):
            acc = None
            for tap in range(CONV_WIDTH):
                r0 = i * rows + 8 * (tap // 8)
                term = dww_ref[cc, tap:tap + 1, :] * vbuf[tap % 8, r0:r0 + rows, :]
                acc = term if acc is None else acc + term
            cbuf[cc, i * rows:(i + 1) * rows, :] = acc
        return carry

    lax.fori_loop(0, nc, conv_chunk, 0)
    v = jnp.concatenate([cbuf[cc] for cc in range(nc)], axis=-1) + dwb_ref[...]
    mu = jnp.mean(v, axis=-1, keepdims=True)
    vc = v - mu
    var = jnp.mean(vc * vc, axis=-1, keepdims=True)
    v = _silu(vc * lax.rsqrt(var + NORM_EPS) * lng_ref[...] + lnb_ref[...])
    y = jnp.dot(v.astype(BF16), w2_ref[...], preferred_element_type=F32) + b2_ref[...]
    _mixer_tail(x, y, mod_ref, n2g_ref[...], rw_ref, rb_ref, x_out, h_out, g_out)


def _conv_mixer(x, mod, n1g, w1, b1, dww, dwb, lng, lnb, w2, b2, n2g, rw, rb, ts=512, rows=64):
    b, s, d = x.shape
    nc = d // LANES
    const = lambda i, j: (0, 0)
    row = lambda a: a.reshape(1, -1)
    dww = dww.reshape(CONV_WIDTH, nc, LANES).transpose(1, 0, 2)
    out_specs, out_shape = _tail_out_specs(b, s, d, ts)
    return pl.pallas_call(
        functools.partial(_conv_mixer_kernel, ts=ts, rows=rows),
        grid=(b, s // ts),
        in_specs=[
            pl.BlockSpec((1, ts, d), lambda i, j: (i, j, 0)),
            pl.BlockSpec((1, 6, d), lambda i, j: (i, 0, 0)),
            pl.BlockSpec((1, d), const),
            pl.BlockSpec(w1.shape, const),
            pl.BlockSpec((1, 2 * d), const),
            pl.BlockSpec(dww.shape, lambda i, j: (0, 0, 0)),
            pl.BlockSpec((1, d), const),
            pl.BlockSpec((1, d), const),
            pl.BlockSpec((1, d), const),
            pl.BlockSpec(w2.shape, const),
            pl.BlockSpec((1, d), const),
            pl.BlockSpec((1, d), const),
            pl.BlockSpec(rw.shape, const),
            pl.BlockSpec((N_EXPERTS, 1), const),
        ],
        out_specs=out_specs,
        out_shape=out_shape,
        scratch_shapes=[pltpu.VMEM((nc, ts + CONV_HALO, LANES), F32),
                        pltpu.VMEM((8, ts + CONV_HALO, LANES), F32),
                        pltpu.VMEM((nc, ts, LANES), F32)],
        compiler_params=_params("arbitrary", "arbitrary"),
        name="conv_mixer_router",
    )(x, mod, row(n1g), w1, row(b1), dww, row(dwb), row(lng), row(lnb), w2, row(b2), row(n2g), rw, rb)


MOE_TILE = 512
PAIRS_PER_GROUP = EXPERTS_PER_GROUP * (EXPERTS_PER_GROUP - 1) // 2
PAIR_CLASSES = N_GROUPS * PAIRS_PER_GROUP


def _class_experts():
    lo, hi = [], []
    for g in range(N_GROUPS):
        for a in range(EXPERTS_PER_GROUP):
            for b in range(a + 1, EXPERTS_PER_GROUP):
                lo.append(g * EXPERTS_PER_GROUP + a)
                hi.append(g * EXPERTS_PER_GROUP + b)
    return jnp.asarray(lo, jnp.int32), jnp.asarray(hi, jnp.int32)


def _dispatch_plan(route, tm):
    t = route.shape[1]
    n_tiles_max = t // tm + PAIR_CLASSES
    i32 = jnp.int32
    e_lo, e_hi = route[0].astype(i32), route[1].astype(i32)
    a, b = e_lo % EXPERTS_PER_GROUP, e_hi % EXPERTS_PER_GROUP
    pair = a * (2 * EXPERTS_PER_GROUP - 1 - a) // 2 + (b - a - 1)
    cls = (e_lo // EXPERTS_PER_GROUP) * PAIRS_PER_GROUP + pair
    onehot = (cls[:, None] == jnp.arange(PAIR_CLASSES, dtype=i32)).astype(i32)
    blocks = onehot.reshape(t // tm, tm, PAIR_CLASSES).astype(F32)
    within = jnp.einsum("ij,bjc->bic", jnp.tril(jnp.ones((tm, tm), F32)), blocks)
    totals = within[:, -1, :]
    before = jnp.cumsum(totals, axis=0) - totals
    csum = (within + before[:, None, :]).reshape(t, PAIR_CLASSES).astype(i32)
    cnt = csum[-1]
    padded = (cnt + tm - 1) // tm * tm
    pad_end = jnp.cumsum(padded)
    off = pad_end - padded
    pos = jnp.sum(onehot * (csum - onehot + off[None, :]), axis=-1)
    n_tiles = (pad_end[-1] // tm).astype(i32)
    tile = jnp.minimum(jnp.arange(n_tiles_max, dtype=i32), n_tiles - 1)
    tile_class = jnp.sum((tile[:, None] >= (pad_end // tm)[None, :]).astype(i32), axis=1)
    class_lo, class_hi = _class_experts()
    return pos.reshape(t // tm, 1, tm), class_lo[tile_class], class_hi[tile_class], n_tiles.reshape(1)


def _gather_pipeline_step(step, n_steps, idx_hbm, src_hbm, idx_smem, buf, isem, gsem, rows):
    n_steps = jnp.asarray(n_steps, jnp.int32)

    def idx_copy(s, sl):
        return pltpu.make_async_copy(idx_hbm.at[s], idx_smem.at[sl], isem.at[sl])

    def start_rows(sl):
        def body(g, carry):
            base = pl.multiple_of(g * SUBLANES, SUBLANES)
            for j in range(SUBLANES):
                row = idx_smem[sl, 0, base + j]
                pltpu.make_async_copy(src_hbm.at[row >> 3, pl.ds(row & (SUBLANES - 1), 1)],
                                      buf.at[sl, g, pl.ds(j, 1)], gsem.at[sl]).start()
            return carry

        lax.fori_loop(0, rows // SUBLANES, body, 0)

    def pipeline(slot):
        nslot = 1 - slot

        @pl.when(step == 0)
        def _():
            first = idx_copy(0, 0)
            first.start()
            first.wait()
            start_rows(0)

            @pl.when(n_steps > 1)
            def _():
                idx_copy(1, 1).start()

        @pl.when(step + 1 < n_steps)
        def _():
            idx_copy(step + 1, nslot).wait()
            start_rows(nslot)

            @pl.when(step + 2 < n_steps)
            def _():
                idx_copy(step + 2, slot).start()

        @pl.when(step < n_steps)
        def _():
            pltpu.make_async_copy(src_hbm.at[pl.ds(0, rows // SUBLANES)], buf.at[slot], gsem.at[slot]).wait()

    for parity in range(2):
        pl.when(step % 2 == parity)(functools.partial(pipeline, parity))


def _gather_scratch(rows, words):
    return [pltpu.SMEM((2, 1, rows), jnp.int32), pltpu.VMEM((2, rows // SUBLANES, SUBLANES, words), jnp.uint32),
            pltpu.SemaphoreType.DMA((2,)), pltpu.SemaphoreType.DMA((2,))]


def _moe_dispatch_kernel(pos_hbm, h_ref, xs_zero_hbm, xs_hbm, idx_smem, hbuf, isem, ssem, *, tm, n_steps):
    del xs_zero_hbm
    i = pl.program_id(0) * pl.num_programs(1) + pl.program_id(1)
    groups = tm // SUBLANES

    def idx_copy(s, sl):
        return pltpu.make_async_copy(pos_hbm.at[s], idx_smem.at[sl], isem.at[sl])

    def wait_rows(sl):
        pltpu.make_async_copy(hbuf.at[sl], xs_hbm.at[pl.ds(0, groups)], ssem.at[sl]).wait()

    def step(slot):
        nslot = 1 - slot

        @pl.when(i == 0)
        def _():
            idx_copy(0, 0).start()

        idx_copy(i, slot).wait()

        @pl.when(i + 1 < n_steps)
        def _():
            idx_copy(i + 1, nslot).start()

        hbuf[slot] = h_ref[0].reshape(groups, SUBLANES, -1)

        def body(g, carry):
            base = pl.multiple_of(g * SUBLANES, SUBLANES)
            for j in range(SUBLANES):
                row = idx_smem[slot, 0, base + j]
                pltpu.make_async_copy(hbuf.at[slot, g, pl.ds(j, 1)],
                                      xs_hbm.at[row >> 3, pl.ds(row & (SUBLANES - 1), 1)],
                                      ssem.at[slot]).start()
            return carry

        lax.fori_loop(0, groups, body, 0)

        @pl.when(i > 0)
        def _():
            wait_rows(nslot)

        @pl.when(i == n_steps - 1)
        def _():
            wait_rows(slot)

    for parity in range(2):
        pl.when(i % 2 == parity)(functools.partial(step, parity))


def _moe_dispatch(pos_tiles, h_words, n_tiles_max, tm):
    b, s, words = h_words.shape
    groups = tm // SUBLANES
    xs_shape = (n_tiles_max * groups, SUBLANES, words)
    return pl.pallas_call(
        functools.partial(_moe_dispatch_kernel, tm=tm, n_steps=b * s // tm),
        grid=(b, s // tm),
        in_specs=[
            pl.BlockSpec(memory_space=pl.ANY),
            pl.BlockSpec((1, tm, words), lambda i, j: (i, j, 0)),
            pl.BlockSpec(memory_space=pl.ANY),
        ],
        out_specs=pl.BlockSpec(memory_space=pl.ANY),
        out_shape=jax.ShapeDtypeStruct(xs_shape, jnp.uint32),
        input_output_aliases={2: 0},
        scratch_shapes=[pltpu.SMEM((2, 1, tm), jnp.int32),
                        pltpu.VMEM((2, groups, SUBLANES, words), jnp.uint32),
                        pltpu.SemaphoreType.DMA((2,)), pltpu.SemaphoreType.DMA((2,))],
        compiler_params=_params("arbitrary", "arbitrary"),
        name="moe_dispatch",
    )(pos_tiles, h_words, jnp.zeros(xs_shape, jnp.uint32))


def _moe_expert_kernel(lo_ref, hi_ref, nt_ref, x_ref, wgu_lo_ref, wgu_hi_ref, wd_lo_ref, wd_hi_ref, y_ref):
    del lo_ref, hi_ref
    i = pl.program_id(0)
    ff = wd_lo_ref.shape[1]
    half = wgu_lo_ref.shape[1] // 2

    @pl.when(i < nt_ref[0])
    def _():
        words = x_ref[...]
        x = _unpack_bf16_pairs(words[:, :half]).astype(BF16)
        meta = lax.bitcast_convert_type(words[:, half:], F32)
        y = None
        for slot, (wgu_ref, wd_ref) in enumerate(((wgu_lo_ref, wd_lo_ref), (wgu_hi_ref, wd_hi_ref))):
            gu = jnp.dot(x, wgu_ref[0], preferred_element_type=F32)
            a = _silu(gu[:, :ff]) * gu[:, ff:] * meta[:, 2 + slot:3 + slot]
            part = jnp.dot(a.astype(BF16), wd_ref[0], preferred_element_type=F32)
            y = part if y is None else y + part
        y_ref[...] = _pack_bf16_pairs(y)

    @pl.when(i >= nt_ref[0])
    def _():
        y_ref[...] = jnp.zeros(y_ref.shape, jnp.uint32)


def _moe_experts(tile_lo, tile_hi, n_tiles, xs, wgu, wd, tm):
    n_tiles_max = tile_lo.shape[0]
    words = xs.shape[1]
    d, ff = wgu.shape[1], wd.shape[1]
    grid_spec = pltpu.PrefetchScalarGridSpec(
        num_scalar_prefetch=3,
        grid=(n_tiles_max,),
        in_specs=[
            pl.BlockSpec((tm, words), lambda i, lo, hi, nt: (i, 0)),
            pl.BlockSpec((1, d, 2 * ff), lambda i, lo, hi, nt: (lo[i], 0, 0)),
            pl.BlockSpec((1, d, 2 * ff), lambda i, lo, hi, nt: (hi[i], 0, 0)),
            pl.BlockSpec((1, ff, d), lambda i, lo, hi, nt: (lo[i], 0, 0)),
            pl.BlockSpec((1, ff, d), lambda i, lo, hi, nt: (hi[i], 0, 0)),
        ],
        out_specs=pl.BlockSpec((tm, d // 2), lambda i, lo, hi, nt: (i, 0)),
    )
    return pl.pallas_call(
        _moe_expert_kernel,
        grid_spec=grid_spec,
        out_shape=jax.ShapeDtypeStruct((n_tiles_max * tm, d // 2), jnp.uint32),
        compiler_params=_params("arbitrary"),
        name="moe_experts",
    )(tile_lo, tile_hi, n_tiles, xs, wgu, wgu, wd, wd)


def _moe_combine_kernel(pos_hbm, y_hbm, x_ref, mod_ref, o_ref, idx_smem, ybuf, isem, gsem, *, tm, n_steps):
    i = pl.program_id(0) * pl.num_programs(1) + pl.program_id(1)
    _gather_pipeline_step(i, n_steps, pos_hbm, y_hbm, idx_smem, ybuf, isem, gsem, tm)
    o_ref[0] = x_ref[0] + mod_ref[0, 5:6, :] * _unpack_bf16_pairs(ybuf[i % 2].reshape(tm, -1))


def _moe_combine(pos_tiles, y_words, x, mod, tm):
    b, s, d = x.shape
    n_steps = b * s // tm
    words = y_words.shape[1]
    return pl.pallas_call(
        functools.partial(_moe_combine_kernel, tm=tm, n_steps=n_steps),
        grid=(b, s // tm),
        in_specs=[
            pl.BlockSpec(memory_space=pl.ANY),
            pl.BlockSpec(memory_space=pl.ANY),
            pl.BlockSpec((1, tm, d), lambda i, j: (i, j, 0)),
            pl.BlockSpec((1, 6, d), lambda i, j: (i, 0, 0)),
        ],
        out_specs=pl.BlockSpec((1, tm, d), lambda i, j: (i, j, 0)),
        out_shape=jax.ShapeDtypeStruct((b, s, d), F32),
        scratch_shapes=_gather_scratch(tm, words),
        compiler_params=_params("arbitrary", "arbitrary"),
        name="moe_combine",
    )(pos_tiles, y_words.reshape(-1, SUBLANES, words), x, mod)


def _moe_layer(h_words, route, x1, mod, w_gate, w_up, w_down):
    pos_tiles, tile_lo, tile_hi, n_tiles = _dispatch_plan(route, MOE_TILE)
    wgu = jnp.concatenate([w_gate, w_up], axis=-1).astype(BF16)
    xs = _moe_dispatch(pos_tiles, h_words, tile_lo.shape[0], MOE_TILE)
    xs = xs.reshape(-1, xs.shape[-1])
    y_words = _moe_experts(tile_lo, tile_hi, n_tiles, xs, wgu, w_down.astype(BF16), MOE_TILE)
    return _moe_combine(pos_tiles, y_words, x1, mod, MOE_TILE)


def kernel(x, c, positions, ada_w, ada_b, norm1_g, norm2_g, mla_w_dqkv, mla_q_norm_g, mla_w_uq, mla_kv_norm_g, mla_w_ukv, mla_qk_q_g, mla_qk_k_g, mla_w_o, conv_pw1_w, conv_pw1_b, conv_dw_w, conv_dw_b, conv_ln_g, conv_ln_b, conv_pw2_w, conv_pw2_b, router_w, router_bias, moe_w_gate, moe_w_up, moe_w_down):
    b, s, d = x.shape
    depth = ada_w.shape[0]
    mod_all = _adaln(c, ada_w, ada_b).reshape(depth, b, 6, d)

    rw_t = router_w.astype(F32).T
    rw_hi = rw_t.astype(BF16)
    rw_lo = (rw_t - rw_hi.astype(F32)).astype(BF16)
    rw = jnp.concatenate([rw_hi, rw_lo], axis=0)
    rb = router_bias.astype(F32).reshape(N_EXPERTS, 1)

    for i in range(depth):
        mod = mod_all[i]
        jdx = i // 2
        if i % 2 == 0:
            w_uq = mla_w_uq[jdx].reshape(Q_LORA, N_HEADS, QK_HEAD)
            wuq = jnp.concatenate([w_uq[:, :, :QK_NOPE].reshape(Q_LORA, -1),
                                   w_uq[:, :, QK_NOPE:].reshape(Q_LORA, -1)], axis=-1).astype(BF16)
            w_ukv = mla_w_ukv[jdx].reshape(KV_LORA, N_HEADS, QK_NOPE + V_HEAD)
            wuk = w_ukv[:, :, :QK_NOPE].reshape(KV_LORA, -1).astype(BF16)
            wuvt = w_ukv[:, :, QK_NOPE:].reshape(KV_LORA, -1).T.astype(BF16)
            q, k, vt = _mla_proj(x, mod, norm1_g[i], mla_w_dqkv[jdx].astype(BF16), mla_q_norm_g[jdx], wuq,
                                 mla_kv_norm_g[jdx], wuk, wuvt, mla_qk_q_g[jdx], mla_qk_k_g[jdx], positions,
                                 ts=PROJ_TILE, vt_tile=ATTN_TILE)
            o = _flash_attention(q, k, vt, t=ATTN_TILE)
            x1, h2, gates_rows = _attn_out(o, x, mod, mla_w_o[jdx].astype(BF16), norm2_g[i], rw, rb)
        else:
            x1, h2, gates_rows = _conv_mixer(
                x, mod, norm1_g[i], conv_pw1_w[jdx].astype(BF16), conv_pw1_b[jdx], conv_dw_w[jdx],
                conv_dw_b[jdx], conv_ln_g[jdx], conv_ln_b[jdx], conv_pw2_w[jdx].astype(BF16),
                conv_pw2_b[jdx], norm2_g[i], rw, rb)
        x = _moe_layer(h2, gates_rows, x1, mod, moe_w_gate[i], moe_w_up[i], moe_w_down[i])
    return x
```

```python
import functools

import jax
import jax.numpy as jnp
from jax import lax
from jax.experimental import pallas as pl
from jax.experimental.pallas import tpu as pltpu

F32 = jnp.float32
BF16 = jnp.bfloat16

N_HEADS = 8
QK_NOPE = 128
QK_ROPE = 64
QK_HEAD = QK_NOPE + QK_ROPE
V_HEAD = 128
Q_LORA = 768
KV_LORA = 256
ROPE_THETA = 10000.0
CONV_WIDTH = 31
CONV_HALO = 32
N_EXPERTS = 16
N_GROUPS = 4
EXPERTS_PER_GROUP = N_EXPERTS // N_GROUPS
TOP_K = 2
ROUTE_ROWS = 8
NORM_EPS = 1e-6
LOG2_E = 1.4426950408889634
LANES = 128
SUBLANES = 8
PROJ_TILE = 256
ATTN_TILE = 512
VMEM_LIMIT_BYTES = 52 * 1024 * 1024


def _params(*semantics):
    return pltpu.CompilerParams(dimension_semantics=semantics, vmem_limit_bytes=VMEM_LIMIT_BYTES)


def _rms(x):
    return x * lax.rsqrt(jnp.mean(x * x, axis=-1, keepdims=True) + NORM_EPS)


def _silu(x):
    return x * jax.nn.sigmoid(x)


def _modulated_norm(x, g, scale, shift):
    return (_rms(x) * g) * (1.0 + scale) + shift


def _adaln_kernel(c_ref, w_ref, b_ref, o_ref):
    a = _silu(c_ref[...]).astype(BF16)
    o_ref[0] = jnp.dot(a, w_ref[0].astype(BF16), preferred_element_type=F32) + b_ref[0]


def _adaln(c, ada_w, ada_b, tn=1536):
    depth, d, n = ada_w.shape
    b = c.shape[0]
    return pl.pallas_call(
        _adaln_kernel,
        grid=(depth, n // tn),
        in_specs=[
            pl.BlockSpec((b, d), lambda l, j: (0, 0)),
            pl.BlockSpec((1, d, tn), lambda l, j: (l, 0, j)),
            pl.BlockSpec((1, 1, tn), lambda l, j: (l, 0, j)),
        ],
        out_specs=pl.BlockSpec((1, b, tn), lambda l, j: (l, 0, j)),
        out_shape=jax.ShapeDtypeStruct((depth, b, n), F32),
        compiler_params=_params("arbitrary", "arbitrary"),
        name="adaln",
    )(c, ada_w, ada_b.reshape(depth, 1, n))


def _rope128(x, cos, sin_signed, lo_mask):
    swapped = jnp.where(lo_mask, pltpu.roll(x, LANES - QK_ROPE // 2, 1), pltpu.roll(x, QK_ROPE // 2, 1))
    return x * cos + swapped * sin_signed


def _mla_proj_kernel(x_ref, mod_ref, n1g_ref, wd_ref, qg_ref, wuq_ref, kvg_ref, wuk_ref, wuvt_ref,
                     gq_ref, gk_ref, pos_ref, invf_ref, q_out, k_out, vt_out):
    x = x_ref[0]
    h = _modulated_norm(x, n1g_ref[...], mod_ref[0, 1:2, :], mod_ref[0, 0:1, :]).astype(BF16)
    lat = jnp.dot(h, wd_ref[...], preferred_element_type=F32)
    c_q = (_rms(lat[:, :Q_LORA]) * qg_ref[...]).astype(BF16)
    c_kv = (_rms(lat[:, Q_LORA:Q_LORA + KV_LORA]) * kvg_ref[...]).astype(BF16)
    k_pe = lat[:, Q_LORA + KV_LORA:]
    q = jnp.dot(c_q, wuq_ref[...], preferred_element_type=F32)
    k_nope = jnp.dot(c_kv, wuk_ref[...], preferred_element_type=F32)
    v_t = lax.dot_general(wuvt_ref[...], c_kv, (((1,), (1,)), ((), ())), preferred_element_type=F32)

    ang = pos_ref[0].astype(F32) * invf_ref[...]
    cos = jnp.cos(ang)
    lane = lax.broadcasted_iota(jnp.int32, (1, LANES), 1)
    lo_mask = (lane % QK_ROPE) < (QK_ROPE // 2)
    sin_signed = jnp.where(lo_mask, -1.0, 1.0) * jnp.sin(ang)

    gq = gq_ref[...]
    gk = gk_ref[...]
    gq_rope2 = jnp.concatenate([gq[:, QK_NOPE:], gq[:, QK_NOPE:]], axis=-1)
    gk_rope2 = jnp.concatenate([gk[:, QK_NOPE:], gk[:, QK_NOPE:]], axis=-1)
    sm_scale = QK_HEAD ** -0.5 * LOG2_E
    nope_all = N_HEADS * QK_NOPE

    k_pe2 = jnp.concatenate([k_pe, k_pe], axis=-1)
    k_rope = _rope128(k_pe2 * gk_rope2, cos, sin_signed, lo_mask)[:, :QK_ROPE]
    ss_pe = jnp.sum(k_pe * k_pe, axis=-1, keepdims=True)

    for pair in range(N_HEADS // 2):
        q_rope_pair = q[:, nope_all + pair * LANES: nope_all + (pair + 1) * LANES]
        q_roped_pair = _rope128(q_rope_pair * gq_rope2, cos, sin_signed, lo_mask)
        for sub in range(2):
            hd = 2 * pair + sub
            qn = q[:, hd * QK_NOPE:(hd + 1) * QK_NOPE]
            qr = q_rope_pair[:, sub * QK_ROPE:(sub + 1) * QK_ROPE]
            ss = jnp.sum(qn * qn, axis=-1, keepdims=True) + jnp.sum(qr * qr, axis=-1, keepdims=True)
            r = lax.rsqrt(ss * (1.0 / QK_HEAD) + NORM_EPS) * sm_scale
            q_out[0, hd, :, 0:QK_NOPE] = (qn * r * gq[:, :QK_NOPE]).astype(BF16)
            q_out[0, hd, :, QK_NOPE:QK_HEAD] = (
                q_roped_pair[:, sub * QK_ROPE:(sub + 1) * QK_ROPE] * r).astype(BF16)

            kn = k_nope[:, hd * QK_NOPE:(hd + 1) * QK_NOPE]
            rk = lax.rsqrt((jnp.sum(kn * kn, axis=-1, keepdims=True) + ss_pe) * (1.0 / QK_HEAD) + NORM_EPS)
            k_out[0, hd, :, 0:QK_NOPE] = (kn * rk * gk[:, :QK_NOPE]).astype(BF16)
            k_out[0, hd, :, QK_NOPE:QK_HEAD] = (k_rope * rk).astype(BF16)
            vt_out[0, hd, 0] = v_t[hd * V_HEAD:(hd + 1) * V_HEAD, :].astype(BF16)


def _mla_proj(x, mod, n1g, wd, qg, wuq, kvg, wuk, wuvt, gq, gk, positions, ts, vt_tile):
    b, s, d = x.shape
    per = vt_tile // ts
    half = QK_ROPE // 2
    inv_freq = 1.0 / (ROPE_THETA ** (jnp.arange(half, dtype=F32) * (2.0 / QK_ROPE)))
    invf = jnp.tile(inv_freq, LANES // half).reshape(1, LANES)
    const = lambda i, j: (0, 0)
    return pl.pallas_call(
        _mla_proj_kernel,
        grid=(b, s // ts),
        in_specs=[
            pl.BlockSpec((1, ts, d), lambda i, j: (i, j, 0)),
            pl.BlockSpec((1, 6, d), lambda i, j: (i, 0, 0)),
            pl.BlockSpec((1, d), const),
            pl.BlockSpec(wd.shape, const),
            pl.BlockSpec((1, Q_LORA), const),
            pl.BlockSpec(wuq.shape, const),
            pl.BlockSpec((1, KV_LORA), const),
            pl.BlockSpec(wuk.shape, const),
            pl.BlockSpec(wuvt.shape, const),
            pl.BlockSpec((1, QK_HEAD), const),
            pl.BlockSpec((1, QK_HEAD), const),
            pl.BlockSpec((1, ts, 1), lambda i, j: (i, j, 0)),
            pl.BlockSpec((1, LANES), const),
        ],
        out_specs=[
            pl.BlockSpec((1, N_HEADS, ts, QK_HEAD), lambda i, j: (i, 0, j, 0)),
            pl.BlockSpec((1, N_HEADS, ts, QK_HEAD), lambda i, j: (i, 0, j, 0)),
            pl.BlockSpec((1, N_HEADS, 1, V_HEAD, ts), lambda i, j: (i, 0, j // per, 0, j % per)),
        ],
        out_shape=[
            jax.ShapeDtypeStruct((b, N_HEADS, s, QK_HEAD), BF16),
            jax.ShapeDtypeStruct((b, N_HEADS, s, QK_HEAD), BF16),
            jax.ShapeDtypeStruct((b, N_HEADS, s // vt_tile, V_HEAD, vt_tile), BF16),
        ],
        compiler_params=_params("arbitrary", "arbitrary"),
        name="mla_proj",
    )(x, mod, n1g.reshape(1, d), wd, qg.reshape(1, -1), wuq, kvg.reshape(1, -1), wuk, wuvt,
      gq.reshape(1, -1), gk.reshape(1, -1), positions.reshape(b, s, 1), invf)


def _flash_kernel(q_ref, k_ref, vt_ref, o_ref, m_ref, l_ref, acc_ref, *stage_refs, t):
    qi = pl.program_id(2)
    heads = q_ref.shape[1]
    s_refs, mb_refs = stage_refs[:heads], stage_refs[heads:]
    nt = (((1,), (1,)), ((), ()))

    def scores(kb, hh, masked, slot=None):
        slot = kb % 2 if slot is None else slot
        start = pl.multiple_of(kb * t, t)
        s = lax.dot_general(k_ref[0, hh, pl.ds(start, t), :], q_ref[0, hh], nt,
                            preferred_element_type=F32)
        if masked:
            kv_pos = lax.broadcasted_iota(jnp.int32, (t, t), 0)
            q_pos = lax.broadcasted_iota(jnp.int32, (t, t), 1)
            s = jnp.where(kv_pos <= q_pos, s, -jnp.inf)
        s_refs[hh][slot] = s
        mb_refs[hh][slot] = jnp.max(s, axis=0, keepdims=True)

    def softmax_pv(kb, hh, slot=None):
        slot = kb % 2 if slot is None else slot
        m = m_ref[hh]
        m_new = jnp.maximum(m, mb_refs[hh][slot])
        alpha = jnp.exp2(m - m_new)
        p = jnp.exp2(s_refs[hh][slot] - m_new)
        l_ref[hh] = alpha * l_ref[hh] + jnp.sum(p, axis=0, keepdims=True)
        m_ref[hh] = m_new
        pv = jnp.dot(vt_ref[0, hh, kb], p.astype(BF16), preferred_element_type=F32)
        acc_ref[hh] = alpha * acc_ref[hh] + pv

    def stage(softmax_kb, scores_kb, masked, softmax_slot=None):
        scores_slot = None if softmax_slot is None else 1 - softmax_slot
        for hh in range(heads):
            if softmax_kb is not None:
                softmax_pv(softmax_kb, hh, softmax_slot)
            if scores_kb is not None:
                scores(scores_kb, hh, masked, scores_slot)

    m_ref[...] = jnp.full(m_ref.shape, -jnp.inf, F32)
    l_ref[...] = jnp.zeros(l_ref.shape, F32)
    acc_ref[...] = jnp.zeros(acc_ref.shape, F32)

    @pl.when(qi == 0)
    def _():
        stage(None, qi, True)

    @pl.when(qi > 0)
    def _():
        stage(None, 0, False)

        steady = qi - 1

        def body(pair, carry):
            kb = 2 * pair
            stage(kb, kb + 1, False, softmax_slot=0)
            stage(kb + 1, kb + 2, False, softmax_slot=1)
            return carry

        lax.fori_loop(0, steady // 2, body, 0)

        @pl.when(steady % 2 == 1)
        def _():
            stage(steady - 1, steady, False)

        stage(qi - 1, qi, True)

    stage(qi, None, False)
    for hh in range(heads):
        o_ref[0, :, hh * V_HEAD:(hh + 1) * V_HEAD] = (acc_ref[hh] * (1.0 / l_ref[hh])).T.astype(BF16)


def _flash_attention(q, k, vt, t, heads=4):
    b, h, s, dqk = q.shape
    dv = vt.shape[-2]
    assert vt.shape[-1] == t and h % heads == 0
    return pl.pallas_call(
        functools.partial(_flash_kernel, t=t),
        grid=(b, h // heads, s // t),
        in_specs=[
            pl.BlockSpec((1, heads, t, dqk), lambda i, j, n: (i, j, n, 0)),
            pl.BlockSpec((1, heads, s, dqk), lambda i, j, n: (i, j, 0, 0), pipeline_mode=pl.Buffered(1)),
            pl.BlockSpec((1, heads, s // t, dv, t), lambda i, j, n: (i, j, 0, 0, 0),
                         pipeline_mode=pl.Buffered(1)),
        ],
        out_specs=pl.BlockSpec((1, t, heads * dv), lambda i, j, n: (i, n, j)),
        out_shape=jax.ShapeDtypeStruct((b, s, h * dv), BF16),
        compiler_params=_params("arbitrary", "arbitrary", "arbitrary"),
        scratch_shapes=(
            [pltpu.VMEM((heads, 1, t), F32),
             pltpu.VMEM((heads, 1, t), F32),
             pltpu.VMEM((heads, dv, t), F32)]
            + [pltpu.VMEM((2, t, t), F32)] * heads
            + [pltpu.VMEM((2, 1, t), F32)] * heads),
        name="flash_attention",
    )(q, k, vt)


def _route_rows(logits, bias_col):
    scores = jax.nn.sigmoid(logits)
    biased = scores + bias_col
    b_rows = [biased[i:i + 1, :] for i in range(N_EXPERTS)]
    s_rows = [scores[i:i + 1, :] for i in range(N_EXPERTS)]
    in_top = []
    group_score = []
    for g in range(N_GROUPS):
        grp = b_rows[g * EXPERTS_PER_GROUP:(g + 1) * EXPERTS_PER_GROUP]
        gs = None
        for i in range(EXPERTS_PER_GROUP):
            rank = None
            for j in range(EXPERTS_PER_GROUP):
                if j == i:
                    continue
                beats = (grp[j] >= grp[i]) if j < i else (grp[j] > grp[i])
                cnt = jnp.where(beats, 1.0, 0.0)
                rank = cnt if rank is None else rank + cnt
            sel = rank < float(TOP_K)
            in_top.append(sel)
            term = jnp.where(sel, grp[i], 0.0)
            gs = term if gs is None else gs + term
        group_score.append(gs)
    picked = []
    for g in range(N_GROUPS):
        lost = None
        for g2 in range(N_GROUPS):
            if g2 == g:
                continue
            beats = (group_score[g2] >= group_score[g]) if g2 < g else (group_score[g2] > group_score[g])
            cnt = jnp.where(beats, 1.0, 0.0)
            lost = cnt if lost is None else lost + cnt
        picked.append(lost < 1.0)
    e_lo = e_hi = w_lo = w_hi = denom = None
    for i in range(N_EXPERTS):
        sel = jnp.where(picked[i // EXPERTS_PER_GROUP], jnp.where(in_top[i], 1.0, 0.0), 0.0)
        w = sel * s_rows[i]
        if i == 0:
            seen = sel
            e_lo, e_hi, w_lo, w_hi, denom = jnp.zeros_like(w), jnp.zeros_like(w), w, jnp.zeros_like(w), w
        else:
            first = sel * (1.0 - seen)
            second = sel * seen
            e_lo = e_lo + first * float(i)
            e_hi = e_hi + second * float(i)
            w_lo = w_lo + first * s_rows[i]
            w_hi = w_hi + second * s_rows[i]
            denom = denom + w
            seen = jnp.maximum(seen, sel)
    inv = 1.0 / denom
    pad = jnp.zeros((ROUTE_ROWS - 4,) + e_lo.shape[1:], F32)
    return jnp.concatenate([e_lo, e_hi, w_lo * inv, w_hi * inv, pad], axis=0)


def _split_bf16(x):
    hi = x.astype(BF16)
    lo = (x - hi.astype(F32)).astype(BF16)
    return hi, lo


def _pack_bf16_pairs(x):
    half = x.shape[-1] // 2
    bits = lax.bitcast_convert_type(x.astype(BF16).astype(F32), jnp.uint32)
    return (bits[:, :half] >> 16) | (bits[:, half:] & jnp.uint32(0xFFFF0000))


def _unpack_bf16_pairs(words):
    lo = lax.bitcast_convert_type(words << 16, F32)
    hi = lax.bitcast_convert_type(words & jnp.uint32(0xFFFF0000), F32)
    return jnp.concatenate([lo, hi], axis=-1)


def _mixer_tail(x, y, mod_ref, n2g, rw_ref, rb_ref, x_out, h_out, g_out):
    x1 = x + mod_ref[0, 2:3, :] * y
    x_out[0] = x1
    h2 = _modulated_norm(x1, n2g, mod_ref[0, 4:5, :], mod_ref[0, 3:4, :])
    h_hi, h_lo = _split_bf16(h2)
    half = h2.shape[-1] // 2
    h_out[0, :, 0:half] = _pack_bf16_pairs(h2)
    nt = (((1,), (1,)), ((), ()))
    a = lax.dot_general(rw_ref[...], h_hi, nt, preferred_element_type=F32)
    c = lax.dot_general(rw_ref[0:N_EXPERTS, :], h_lo, nt, preferred_element_type=F32)
    logits = a[0:N_EXPERTS, :] + a[N_EXPERTS:, :] + c
    route = _route_rows(logits, rb_ref[...])
    g_out[...] = route
    pad = jnp.zeros((LANES - ROUTE_ROWS, route.shape[1]), F32)
    meta = jnp.concatenate([route, pad], axis=0).T
    h_out[0, :, half:half + LANES] = lax.bitcast_convert_type(meta, jnp.uint32)


def _attn_out_kernel(o_ref, x_ref, mod_ref, wo_ref, n2g_ref, rw_ref, rb_ref, x_out, h_out, g_out):
    y = jnp.dot(o_ref[0], wo_ref[...], preferred_element_type=F32)
    _mixer_tail(x_ref[0], y, mod_ref, n2g_ref[...], rw_ref, rb_ref, x_out, h_out, g_out)


def _tail_out_specs(b, s, d, tm):
    nt = s // tm
    specs = [
        pl.BlockSpec((1, tm, d), lambda i, j: (i, j, 0)),
        pl.BlockSpec((1, tm, d // 2 + LANES), lambda i, j: (i, j, 0)),
        pl.BlockSpec((ROUTE_ROWS, tm), lambda i, j: (0, i * nt + j)),
    ]
    shapes = [
        jax.ShapeDtypeStruct((b, s, d), F32),
        jax.ShapeDtypeStruct((b, s, d // 2 + LANES), jnp.uint32),
        jax.ShapeDtypeStruct((ROUTE_ROWS, b * s), F32),
    ]
    return specs, shapes


def _attn_out(o, x, mod, wo, n2g, rw, rb, tm=512):
    b, s, d = x.shape
    const = lambda i, j: (0, 0)
    out_specs, out_shape = _tail_out_specs(b, s, d, tm)
    return pl.pallas_call(
        _attn_out_kernel,
        grid=(b, s // tm),
        in_specs=[
            pl.BlockSpec((1, tm, o.shape[-1]), lambda i, j: (i, j, 0)),
            pl.BlockSpec((1, tm, d), lambda i, j: (i, j, 0)),
            pl.BlockSpec((1, 6, d), lambda i, j: (i, 0, 0)),
            pl.BlockSpec(wo.shape, const),
            pl.BlockSpec((1, d), const),
            pl.BlockSpec(rw.shape, const),
            pl.BlockSpec((N_EXPERTS, 1), const),
        ],
        out_specs=out_specs,
        out_shape=out_shape,
        compiler_params=_params("arbitrary", "arbitrary"),
        name="attn_out_router",
    )(o, x, mod, wo, n2g.reshape(1, d), rw, rb)


def _conv_mixer_kernel(x_ref, mod_ref, n1g_ref, w1_ref, b1_ref, dww_ref, dwb_ref, lng_ref, lnb_ref,
                       w2_ref, b2_ref, n2g_ref, rw_ref, rb_ref, x_out, h_out, g_out, ubuf, vbuf, cbuf,
                       *, ts, rows):
    j = pl.program_id(1)
    x = x_ref[0]
    d = x.shape[-1]
    h = _modulated_norm(x, n1g_ref[...], mod_ref[0, 1:2, :], mod_ref[0, 0:1, :]).astype(BF16)
    u = jnp.dot(h, w1_ref[...], preferred_element_type=F32) + b1_ref[...]
    u = u[:, :d] * jax.nn.sigmoid(u[:, d:])

    nc = d // LANES

    @pl.when(j == 0)
    def _():
        ubuf[:, 0:CONV_HALO, :] = jnp.zeros((nc, CONV_HALO, LANES), F32)

    @pl.when(j > 0)
    def _():
        ubuf[:, 0:CONV_HALO, :] = ubuf[:, ts:ts + CONV_HALO, :]

    for cc in range(nc):
        ubuf[cc, CONV_HALO:CONV_HALO + ts, :] = u[:, cc * LANES:(cc + 1) * LANES]

    first = CONV_HALO - (CONV_WIDTH - 1)

    def conv_chunk(cc, carry):
        for r in range(8):
            span = ts + 8 * ((CONV_WIDTH - 1 - r) // 8)
            vbuf[r, 0:span, :] = ubuf[cc, first + r:first + r + span, :]
        for i in range(ts // rows):
            acc = None
            for tap in range(CONV_WIDTH):
                r0 = i * rows + 8 * (tap // 8)
                term = dww_ref[cc, tap:tap + 1, :] * vbuf[tap % 8, r0:r0 + rows, :]
                acc = term if acc is None else acc + term
            cbuf[cc, i * rows:(i + 1) * rows, :] = acc
        return carry

    lax.fori_loop(0, nc, conv_chunk, 0)
    v = jnp.concatenate([cbuf[cc] for cc in range(nc)], axis=-1) + dwb_ref[...]
    mu = jnp.mean(v, axis=-1, keepdims=True)
    vc = v - mu
    var = jnp.mean(vc * vc, axis=-1, keepdims=True)
    v = _silu(vc * lax.rsqrt(var + NORM_EPS) * lng_ref[...] + lnb_ref[...])
    y = jnp.dot(v.astype(BF16), w2_ref[...], preferred_element_type=F32) + b2_ref[...]
    _mixer_tail(x, y, mod_ref, n2g_ref[...], rw_ref, rb_ref, x_out, h_out, g_out)


def _conv_mixer(x, mod, n1g, w1, b1, dww, dwb, lng, lnb, w2, b2, n2g, rw, rb, ts=512, rows=64):
    b, s, d = x.shape
    nc = d // LANES
    const = lambda i, j: (0, 0)
    row = lambda a: a.reshape(1, -1)
    dww = dww.reshape(CONV_WIDTH, nc, LANES).transpose(1, 0, 2)
    out_specs, out_shape = _tail_out_specs(b, s, d, ts)
    return pl.pallas_call(
        functools.partial(_conv_mixer_kernel, ts=ts, rows=rows),
        grid=(b, s // ts),
        in_specs=[
            pl.BlockSpec((1, ts, d), lambda i, j: (i, j, 0)),
            pl.BlockSpec((1, 6, d), lambda i, j: (i, 0, 0)),
            pl.BlockSpec((1, d), const),
            pl.BlockSpec(w1.shape, const),
            pl.BlockSpec((1, 2 * d), const),
            pl.BlockSpec(dww.shape, lambda i, j: (0, 0, 0)),
            pl.BlockSpec((1, d), const),
            pl.BlockSpec((1, d), const),
            pl.BlockSpec((1, d), const),
            pl.BlockSpec(w2.shape, const),
            pl.BlockSpec((1, d), const),
            pl.BlockSpec((1, d), const),
            pl.BlockSpec(rw.shape, const),
            pl.BlockSpec((N_EXPERTS, 1), const),
        ],
        out_specs=out_specs,
        out_shape=out_shape,
        scratch_shapes=[pltpu.VMEM((nc, ts + CONV_HALO, LANES), F32),
                        pltpu.VMEM((8, ts + CONV_HALO, LANES), F32),
                        pltpu.VMEM((nc, ts, LANES), F32)],
        compiler_params=_params("arbitrary", "arbitrary"),
        name="conv_mixer_router",
    )(x, mod, row(n1g), w1, row(b1), dww, row(dwb), row(lng), row(lnb), w2, row(b2), row(n2g), rw, rb)


MOE_TILE = 512
PAIRS_PER_GROUP = EXPERTS_PER_GROUP * (EXPERTS_PER_GROUP - 1) // 2
PAIR_CLASSES = N_GROUPS * PAIRS_PER_GROUP


def _class_experts():
    lo, hi = [], []
    for g in range(N_GROUPS):
        for a in range(EXPERTS_PER_GROUP):
            for b in range(a + 1, EXPERTS_PER_GROUP):
                lo.append(g * EXPERTS_PER_GROUP + a)
                hi.append(g * EXPERTS_PER_GROUP + b)
    return jnp.asarray(lo, jnp.int32), jnp.asarray(hi, jnp.int32)


def _dispatch_plan(route, tm):
    t = route.shape[1]
    n_tiles_max = t // tm + PAIR_CLASSES
    i32 = jnp.int32
    e_lo, e_hi = route[0].astype(i32), route[1].astype(i32)
    a, b = e_lo % EXPERTS_PER_GROUP, e_hi % EXPERTS_PER_GROUP
    pair = a * (2 * EXPERTS_PER_GROUP - 1 - a) // 2 + (b - a - 1)
    cls = (e_lo // EXPERTS_PER_GROUP) * PAIRS_PER_GROUP + pair
    onehot = (cls[:, None] == jnp.arange(PAIR_CLASSES, dtype=i32)).astype(i32)
    blocks = onehot.reshape(t // tm, tm, PAIR_CLASSES).astype(F32)
    within = jnp.einsum("ij,bjc->bic", jnp.tril(jnp.ones((tm, tm), F32)), blocks)
    totals = within[:, -1, :]
    before = jnp.cumsum(totals, axis=0) - totals
    csum = (within + before[:, None, :]).reshape(t, PAIR_CLASSES).astype(i32)
    cnt = csum[-1]
    padded = (cnt + tm - 1) // tm * tm
    pad_end = jnp.cumsum(padded)
    off = pad_end - padded
    pos = jnp.sum(onehot * (csum - onehot + off[None, :]), axis=-1)
    n_tiles = (pad_end[-1] // tm).astype(i32)
    tile = jnp.minimum(jnp.arange(n_tiles_max, dtype=i32), n_tiles - 1)
    tile_class = jnp.sum((tile[:, None] >= (pad_end // tm)[None, :]).astype(i32), axis=1)
    class_lo, class_hi = _class_experts()
    return pos.reshape(t // tm, 1, tm), class_lo[tile_class], class_hi[tile_class], n_tiles.reshape(1)


def _gather_pipeline_step(step, n_steps, idx_hbm, src_hbm, idx_smem, buf, isem, gsem, rows):
    n_steps = jnp.asarray(n_steps, jnp.int32)

    def idx_copy(s, sl):
        return pltpu.make_async_copy(idx_hbm.at[s], idx_smem.at[sl], isem.at[sl])

    def start_rows(sl):
        def body(g, carry):
            base = pl.multiple_of(g * SUBLANES, SUBLANES)
            for j in range(SUBLANES):
                row = idx_smem[sl, 0, base + j]
                pltpu.make_async_copy(src_hbm.at[row >> 3, pl.ds(row & (SUBLANES - 1), 1)],
                                      buf.at[sl, g, pl.ds(j, 1)], gsem.at[sl]).start(priority=j % 2)
            return carry

        lax.fori_loop(0, rows // SUBLANES, body, 0)

    def pipeline(slot):
        nslot = 1 - slot

        @pl.when(step == 0)
        def _():
            first = idx_copy(0, 0)
            first.start()
            first.wait()
            start_rows(0)

            @pl.when(n_steps > 1)
            def _():
                idx_copy(1, 1).start()

        @pl.when(step + 1 < n_steps)
        def _():
            idx_copy(step + 1, nslot).wait()
            start_rows(nslot)

            @pl.when(step + 2 < n_steps)
            def _():
                idx_copy(step + 2, slot).start()

        @pl.when(step < n_steps)
        def _():
            pltpu.make_async_copy(src_hbm.at[pl.ds(0, rows // SUBLANES)], buf.at[slot], gsem.at[slot]).wait()

    for parity in range(2):
        pl.when(step % 2 == parity)(functools.partial(pipeline, parity))


def _gather_scratch(rows, words):
    return [pltpu.SMEM((2, 1, rows), jnp.int32), pltpu.VMEM((2, rows // SUBLANES, SUBLANES, words), jnp.uint32),
            pltpu.SemaphoreType.DMA((2,)), pltpu.SemaphoreType.DMA((2,))]


def _moe_dispatch_kernel(pos_hbm, h_ref, xs_zero_hbm, xs_hbm, idx_smem, hbuf, isem, ssem, *, tm, n_steps):
    del xs_zero_hbm
    i = pl.program_id(0) * pl.num_programs(1) + pl.program_id(1)
    groups = tm // SUBLANES

    def idx_copy(s, sl):
        return pltpu.make_async_copy(pos_hbm.at[s], idx_smem.at[sl], isem.at[sl])

    def wait_rows(sl):
        pltpu.make_async_copy(hbuf.at[sl], xs_hbm.at[pl.ds(0, groups)], ssem.at[sl]).wait()

    def step(slot):
        nslot = 1 - slot

        @pl.when(i == 0)
        def _():
            idx_copy(0, 0).start()

        idx_copy(i, slot).wait()

        @pl.when(i + 1 < n_steps)
        def _():
            idx_copy(i + 1, nslot).start()

        hbuf[slot] = h_ref[0].reshape(groups, SUBLANES, -1)

        def body(g, carry):
            base = pl.multiple_of(g * SUBLANES, SUBLANES)
            for j in range(SUBLANES):
                row = idx_smem[slot, 0, base + j]
                pltpu.make_async_copy(hbuf.at[slot, g, pl.ds(j, 1)],
                                      xs_hbm.at[row >> 3, pl.ds(row & (SUBLANES - 1), 1)],
                                      ssem.at[slot]).start(priority=j % 2)
            return carry

        lax.fori_loop(0, groups, body, 0)

        @pl.when(i > 0)
        def _():
            wait_rows(nslot)

        @pl.when(i == n_steps - 1)
        def _():
            wait_rows(slot)

    for parity in range(2):
        pl.when(i % 2 == parity)(functools.partial(step, parity))


def _moe_dispatch(pos_tiles, h_words, n_tiles_max, tm):
    b, s, words = h_words.shape
    groups = tm // SUBLANES
    xs_shape = (n_tiles_max * groups, SUBLANES, words)
    return pl.pallas_call(
        functools.partial(_moe_dispatch_kernel, tm=tm, n_steps=b * s // tm),
        grid=(b, s // tm),
        in_specs=[
            pl.BlockSpec(memory_space=pl.ANY),
            pl.BlockSpec((1, tm, words), lambda i, j: (i, j, 0)),
            pl.BlockSpec(memory_space=pl.ANY),
        ],
        out_specs=pl.BlockSpec(memory_space=pl.ANY),
        out_shape=jax.ShapeDtypeStruct(xs_shape, jnp.uint32),
        input_output_aliases={2: 0},
        scratch_shapes=[pltpu.SMEM((2, 1, tm), jnp.int32),
                        pltpu.VMEM((2, groups, SUBLANES, words), jnp.uint32),
                        pltpu.SemaphoreType.DMA((2,)), pltpu.SemaphoreType.DMA((2,))],
        compiler_params=_params("arbitrary", "arbitrary"),
        name="moe_dispatch",
    )(pos_tiles, h_words, jnp.zeros(xs_shape, jnp.uint32))


def _moe_expert_kernel(lo_ref, hi_ref, nt_ref, x_ref, wgu_lo_ref, wgu_hi_ref, wd_lo_ref, wd_hi_ref, y_ref):
    del lo_ref, hi_ref
    i = pl.program_id(0)
    ff = wd_lo_ref.shape[1]
    half = wgu_lo_ref.shape[1] // 2

    @pl.when(i < nt_ref[0])
    def _():
        words = x_ref[...]
        x = _unpack_bf16_pairs(words[:, :half]).astype(BF16)
        meta = lax.bitcast_convert_type(words[:, half:], F32)
        y = None
        for slot, (wgu_ref, wd_ref) in enumerate(((wgu_lo_ref, wd_lo_ref), (wgu_hi_ref, wd_hi_ref))):
            gu = jnp.dot(x, wgu_ref[0], preferred_element_type=F32)
            a = _silu(gu[:, :ff]) * gu[:, ff:] * meta[:, 2 + slot:3 + slot]
            part = jnp.dot(a.astype(BF16), wd_ref[0], preferred_element_type=F32)
            y = part if y is None else y + part
        y_ref[...] = _pack_bf16_pairs(y)

    @pl.when(i >= nt_ref[0])
    def _():
        y_ref[...] = jnp.zeros(y_ref.shape, jnp.uint32)


def _moe_experts(tile_lo, tile_hi, n_tiles, xs, wgu, wd, tm):
    n_tiles_max = tile_lo.shape[0]
    words = xs.shape[1]
    d, ff = wgu.shape[1], wd.shape[1]
    grid_spec = pltpu.PrefetchScalarGridSpec(
        num_scalar_prefetch=3,
        grid=(n_tiles_max,),
        in_specs=[
            pl.BlockSpec((tm, words), lambda i, lo, hi, nt: (i, 0)),
            pl.BlockSpec((1, d, 2 * ff), lambda i, lo, hi, nt: (lo[i], 0, 0)),
            pl.BlockSpec((1, d, 2 * ff), lambda i, lo, hi, nt: (hi[i], 0, 0)),
            pl.BlockSpec((1, ff, d), lambda i, lo, hi, nt: (lo[i], 0, 0)),
            pl.BlockSpec((1, ff, d), lambda i, lo, hi, nt: (hi[i], 0, 0)),
        ],
        out_specs=pl.BlockSpec((tm, d // 2), lambda i, lo, hi, nt: (i, 0)),
    )
    return pl.pallas_call(
        _moe_expert_kernel,
        grid_spec=grid_spec,
        out_shape=jax.ShapeDtypeStruct((n_tiles_max * tm, d // 2), jnp.uint32),
        compiler_params=_params("arbitrary"),
        name="moe_experts",
    )(tile_lo, tile_hi, n_tiles, xs, wgu, wgu, wd, wd)


def _moe_combine_kernel(pos_hbm, y_hbm, x_ref, mod_ref, o_ref, idx_smem, ybuf, isem, gsem, *, tm, n_steps):
    i = pl.program_id(0) * pl.num_programs(1) + pl.program_id(1)
    _gather_pipeline_step(i, n_steps, pos_hbm, y_hbm, idx_smem, ybuf, isem, gsem, tm)
    o_ref[0] = x_ref[0] + mod_ref[0, 5:6, :] * _unpack_bf16_pairs(ybuf[i % 2].reshape(tm, -1))


def _moe_combine(pos_tiles, y_words, x, mod, tm):
    b, s, d = x.shape
    n_steps = b * s // tm
    words = y_words.shape[1]
    return pl.pallas_call(
        functools.partial(_moe_combine_kernel, tm=tm, n_steps=n_steps),
        grid=(b, s // tm),
        in_specs=[
            pl.BlockSpec(memory_space=pl.ANY),
            pl.BlockSpec(memory_space=pl.ANY),
            pl.BlockSpec((1, tm, d), lambda i, j: (i, j, 0)),
            pl.BlockSpec((1, 6, d), lambda i, j: (i, 0, 0)),
        ],
        out_specs=pl.BlockSpec((1, tm, d), lambda i, j: (i, j, 0)),
        out_shape=jax.ShapeDtypeStruct((b, s, d), F32),
        scratch_shapes=_gather_scratch(tm, words),
        compiler_params=_params("arbitrary", "arbitrary"),
        name="moe_combine",
    )(pos_tiles, y_words.reshape(-1, SUBLANES, words), x, mod)


def _moe_layer(h_words, route, x1, mod, w_gate, w_up, w_down):
    pos_tiles, tile_lo, tile_hi, n_tiles = _dispatch_plan(route, MOE_TILE)
    wgu = jnp.concatenate([w_gate, w_up], axis=-1).astype(BF16)
    xs = _moe_dispatch(pos_tiles, h_words, tile_lo.shape[0], MOE_TILE)
    xs = xs.reshape(-1, xs.shape[-1])
    y_words = _moe_experts(tile_lo, tile_hi, n_tiles, xs, wgu, w_down.astype(BF16), MOE_TILE)
    return _moe_combine(pos_tiles, y_words, x1, mod, MOE_TILE)


def kernel(x, c, positions, ada_w, ada_b, norm1_g, norm2_g, mla_w_dqkv, mla_q_norm_g, mla_w_uq, mla_kv_norm_g, mla_w_ukv, mla_qk_q_g, mla_qk_k_g, mla_w_o, conv_pw1_w, conv_pw1_b, conv_dw_w, conv_dw_b, conv_ln_g, conv_ln_b, conv_pw2_w, conv_pw2_b, router_w, router_bias, moe_w_gate, moe_w_up, moe_w_down):
    b, s, d = x.shape
    depth = ada_w.shape[0]
    mod_all = _adaln(c, ada_w, ada_b).reshape(depth, b, 6, d)

    rw_t = router_w.astype(F32).T
    rw_hi = rw_t.astype(BF16)
    rw_lo = (rw_t - rw_hi.astype(F32)).astype(BF16)
    rw = jnp.concatenate([rw_hi, rw_lo], axis=0)
    rb = router_bias.astype(F32).reshape(N_EXPERTS, 1)

    for i in range(depth):
        mod = mod_all[i]
        jdx = i // 2
        if i % 2 == 0:
            w_uq = mla_w_uq[jdx].reshape(Q_LORA, N_HEADS, QK_HEAD)
            wuq = jnp.concatenate([w_uq[:, :, :QK_NOPE].reshape(Q_LORA, -1),
                                   w_uq[:, :, QK_NOPE:].reshape(Q_LORA, -1)], axis=-1).astype(BF16)
            w_ukv = mla_w_ukv[jdx].reshape(KV_LORA, N_HEADS, QK_NOPE + V_HEAD)
            wuk = w_ukv[:, :, :QK_NOPE].reshape(KV_LORA, -1).astype(BF16)
            wuvt = w_ukv[:, :, QK_NOPE:].reshape(KV_LORA, -1).T.astype(BF16)
            q, k, vt = _mla_proj(x, mod, norm1_g[i], mla_w_dqkv[jdx].astype(BF16), mla_q_norm_g[jdx], wuq,
                                 mla_kv_norm_g[jdx], wuk, wuvt, mla_qk_q_g[jdx], mla_qk_k_g[jdx], positions,
                                 ts=PROJ_TILE, vt_tile=ATTN_TILE)
            o = _flash_attention(q, k, vt, t=ATTN_TILE)
            x1, h2, gates_rows = _attn_out(o, x, mod, mla_w_o[jdx].astype(BF16), norm2_g[i], rw, rb)
        else:
            x1, h2, gates_rows = _conv_mixer(
                x, mod, norm1_g[i], conv_pw1_w[jdx].astype(BF16), conv_pw1_b[jdx], conv_dw_w[jdx],
                conv_dw_b[jdx], conv_ln_g[jdx], conv_ln_b[jdx], conv_pw2_w[jdx].astype(BF16),
                conv_pw2_b[jdx], norm2_g[i], rw, rb)
        x = _moe_layer(h2, gates_rows, x1, mod, moe_w_gate[i], moe_w_up[i], moe_w_down[i])
    return x
```
